```python
import math
import jax, jax.numpy as jnp
from jax import lax
import numpy as np

D_MODEL = 2048
BATCH = 2
SEQ = 4096
DEPTH = 2
DEC_BATCH = 32
DEC_SEQ = 8
PAST_LEN = 16384
PAGE_SIZE = 128

N_A = (DEPTH + 1) // 2
N_B = DEPTH - N_A
N_HEADS = 16
DQK = D_MODEL // N_HEADS // 2
DV = 2 * DQK
HD_B = D_MODEL // N_HEADS
KV_HEADS_B = 4
GROUP_B = N_HEADS // KV_HEADS_B
WINDOW = 128
D_FF = 5632
N_BUCKETS = 32
MAX_DISTANCE = 128
Q_BLOCK = 128
EPS = 1e-6
NEG = -1e30

kernel_name = "yoco_diffattn_swa_sink_macaron_step"


def rms_norm(x, g):
    xf = x.astype(jnp.float32)
    y = xf * lax.rsqrt(jnp.mean(xf * xf, axis=-1, keepdims=True) + EPS)
    return (y * g.astype(jnp.float32)).astype(x.dtype)


def swiglu(x, w_in, w_out):
    a, b = jnp.split(x @ w_in, 2, axis=-1)
    return (jax.nn.silu(a) * b) @ w_out


def rel_bucket(dist):
    n = jnp.maximum(dist, 0)
    max_exact = N_BUCKETS // 2
    nf = jnp.maximum(n, 1).astype(jnp.float32)
    large = max_exact + (jnp.log(nf / max_exact) / math.log(MAX_DISTANCE / max_exact)
                         * (N_BUCKETS - max_exact)).astype(jnp.int32)
    return jnp.where(n < max_exact, n, jnp.minimum(large, N_BUCKETS - 1))


def rel_bias(table, dist):
    return jnp.moveaxis(table.astype(jnp.float32)[rel_bucket(dist)], -1, 0)


def diff_lambda(lam_params, layer_idx):
    lam_init = 0.8 - 0.6 * math.exp(-0.3 * layer_idx)
    lp = lam_params.astype(jnp.float32)
    lam = jnp.exp(jnp.sum(lp[0] * lp[1])) - jnp.exp(jnp.sum(lp[2] * lp[3])) + lam_init
    return lam, lam_init


def diff_attn_prompt(q, k, v, lam, table):
    bsz, s_len = q.shape[:2]
    k_pos = jnp.arange(s_len)
    scale = DQK ** -0.5

    def block(i):
        qb = lax.dynamic_slice_in_dim(q, i * Q_BLOCK, Q_BLOCK, axis=1)
        dist = (i * Q_BLOCK + jnp.arange(Q_BLOCK))[:, None] - k_pos[None, :]
        s = jnp.einsum("bqhcd,bkhcd->bhcqk", qb, k, preferred_element_type=jnp.float32) * scale
        s = jnp.where(dist >= 0, s + rel_bias(table, dist)[None, :, None], NEG)
        p = jax.nn.softmax(s, axis=-1)
        o = jnp.einsum("bhcqk,bkhe->bqhce", p.astype(v.dtype), v, preferred_element_type=jnp.float32)
        return o[..., 0, :] - lam * o[..., 1, :]

    out = lax.map(block, jnp.arange(s_len // Q_BLOCK))
    return jnp.moveaxis(out, 0, 1).reshape(bsz, s_len, N_HEADS, DV)


def diff_attn_sample(q, k_new, v_new, cache_k, cache_v, layer, page_table, lam, table):
    bsz, t_len = q.shape[:2]
    n_pages = page_table.shape[1]
    q_pos = n_pages * PAGE_SIZE + jnp.arange(t_len)
    scale = DQK ** -0.5

    def update(carry, kk, vv, k_pos):
        m, l, acc = carry
        dist = q_pos[:, None] - k_pos[None, :]
        s = jnp.einsum("bqhcd,bkhcd->bhcqk", q, kk, preferred_element_type=jnp.float32) * scale
        s = jnp.where(dist >= 0, s + rel_bias(table, dist)[None, :, None], NEG)
        m_new = jnp.maximum(m, jnp.max(s, axis=-1))
        p = jnp.exp(s - m_new[..., None])
        alpha = jnp.exp(m - m_new)
        l = l * alpha + jnp.sum(p, axis=-1)
        acc = acc * alpha[..., None] + jnp.einsum("bhcqk,bkhe->bhcqe", p.astype(vv.dtype), vv,
                                                  preferred_element_type=jnp.float32)
        return (m_new, l, acc)

    init = (jnp.full((bsz, N_HEADS, 2, t_len), NEG, jnp.float32),
            jnp.zeros((bsz, N_HEADS, 2, t_len), jnp.float32),
            jnp.zeros((bsz, N_HEADS, 2, t_len, DV), jnp.float32))

    def page_step(carry, j):
        phys = page_table[:, j]
        kp = cache_k[layer, phys].reshape(bsz, PAGE_SIZE, N_HEADS, 2, DQK)
        vp = cache_v[layer, phys]
        return update(carry, kp, vp, j * PAGE_SIZE + jnp.arange(PAGE_SIZE)), None

    carry, _ = lax.scan(page_step, init, jnp.arange(n_pages))
    m, l, acc = update(carry, k_new, v_new, q_pos)
    o = acc / l[..., None]
    o = o[:, :, 0] - lam * o[:, :, 1]
    return jnp.transpose(o, (0, 2, 1, 3))


def shared_kv(x, norm_kv, w_b_kv, b_k_norm):
    bsz, t_len = x.shape[:2]
    k, v = jnp.split(rms_norm(x, norm_kv) @ w_b_kv, 2, axis=-1)
    k = rms_norm(k.reshape(bsz, t_len, KV_HEADS_B, HD_B), b_k_norm)
    return k, v.reshape(bsz, t_len, KV_HEADS_B, HD_B)


def sink_attend(s, sinks, v, spec):
    sink = jnp.broadcast_to(sinks.astype(jnp.float32).reshape(KV_HEADS_B, GROUP_B, 1, 1),
                            s.shape[:-1] + (1,))
    p = jax.nn.softmax(jnp.concatenate([s, sink], axis=-1), axis=-1)[..., :-1]
    return jnp.einsum(spec, p.astype(v.dtype), v, preferred_element_type=jnp.float32)


def swa_prompt(q, k, v, sinks, table):
    bsz, s_len = q.shape[:2]
    nb = s_len // WINDOW
    qb = q.reshape(bsz, nb, WINDOW, KV_HEADS_B, GROUP_B, HD_B)
    kb = k.reshape(bsz, nb, WINDOW, KV_HEADS_B, HD_B)
    vb = v.reshape(bsz, nb, WINDOW, KV_HEADS_B, HD_B)
    kk = jnp.concatenate([jnp.concatenate([jnp.zeros_like(kb[:, :1]), kb[:, :-1]], axis=1), kb], axis=2)
    vv = jnp.concatenate([jnp.concatenate([jnp.zeros_like(vb[:, :1]), vb[:, :-1]], axis=1), vb], axis=2)
    si = jnp.arange(2 * WINDOW)
    dist = WINDOW + jnp.arange(WINDOW)[:, None] - si[None, :]
    valid = ((dist >= 0) & (dist < WINDOW))[None] & (
        (jnp.arange(nb) > 0)[:, None, None] | (si >= WINDOW)[None, None, :])
    bias = rel_bias(table, dist).reshape(KV_HEADS_B, GROUP_B, WINDOW, 2 * WINDOW)
    s = jnp.einsum("bnqkgd,bnskd->bnkgqs", qb, kk, preferred_element_type=jnp.float32) * HD_B ** -0.5 + bias
    s = jnp.where(valid[None, :, None, None], s, NEG)
    o = sink_attend(s, sinks, vv, "bnkgqs,bnskd->bnqkgd")
    return o.reshape(bsz, s_len, N_HEADS, HD_B)


def swa_sample(q, kk, vv, sinks, table):
    bsz, t_len = q.shape[:2]
    buf_len = kk.shape[1] - t_len
    dist = jnp.arange(t_len)[:, None] - (jnp.arange(buf_len + t_len) - buf_len)[None, :]
    valid = (dist >= 0) & (dist < WINDOW)
    bias = rel_bias(table, dist).reshape(KV_HEADS_B, GROUP_B, t_len, buf_len + t_len)
    qg = q.reshape(bsz, t_len, KV_HEADS_B, GROUP_B, HD_B)
    s = jnp.einsum("bqkgd,bskd->bkgqs", qg, kk, preferred_element_type=jnp.float32) * HD_B ** -0.5 + bias
    s = jnp.where(valid, s, NEG)
    o = sink_attend(s, sinks, vv, "bkgqs,bskd->bqkgd")
    return o.reshape(bsz, t_len, N_HEADS, HD_B)


def trunk(x, sample, cache_a_k, cache_a_v, cache_b_k, cache_b_v, page_table, rel_bias_table,
          norm_ffn1, norm_attn, norm_ffn2, w_ffn_in, w_ffn_out, w_a_qkv, a_q_norm, a_k_norm,
          a_lambda, a_subln, w_a_o, norm_kv, w_b_kv, b_k_norm, w_b_q, b_q_norm, b_sinks, w_b_o):
    bsz, t_len = x.shape[:2]
    a_ks, a_vs = [], []
    kv_k = kv_v = None
    for li in range(DEPTH):
        if li == N_A:
            kv_k, kv_v = shared_kv(x, norm_kv, w_b_kv, b_k_norm)
            if sample:
                kv_k = jnp.concatenate([cache_b_k, kv_k], axis=1)
                kv_v = jnp.concatenate([cache_b_v, kv_v], axis=1)
        x = x + 0.5 * swiglu(rms_norm(x, norm_ffn1[li]), w_ffn_in[li, 0], w_ffn_out[li, 0])
        h = rms_norm(x, norm_attn[li])
        if li < N_A:
            q, k, v = jnp.split(h @ w_a_qkv[li], [N_HEADS * 2 * DQK, N_HEADS * 4 * DQK], axis=-1)
            q = rms_norm(q.reshape(bsz, t_len, N_HEADS, 2, DQK), a_q_norm[li])
            k = rms_norm(k.reshape(bsz, t_len, N_HEADS, 2, DQK), a_k_norm[li])
            v = v.reshape(bsz, t_len, N_HEADS, DV)
            a_ks.append(k.reshape(bsz, t_len, N_HEADS, 2 * DQK))
            a_vs.append(v)
            lam, lam_init = diff_lambda(a_lambda[li], li)
            if sample:
                o = diff_attn_sample(q, k, v, cache_a_k, cache_a_v, li, page_table, lam, rel_bias_table)
            else:
                o = diff_attn_prompt(q, k, v, lam, rel_bias_table)
            o = rms_norm(o, a_subln[li]) * (1.0 - lam_init)
            mix = o.astype(x.dtype).reshape(bsz, t_len, D_MODEL) @ w_a_o[li]
        else:
            bi = li - N_A
            q = rms_norm((h @ w_b_q[bi]).reshape(bsz, t_len, N_HEADS, HD_B), b_q_norm[bi])
            if sample:
                o = swa_sample(q, kv_k, kv_v, b_sinks[bi], rel_bias_table)
            else:
                o = swa_prompt(q, kv_k, kv_v, b_sinks[bi], rel_bias_table)
            mix = o.astype(x.dtype).reshape(bsz, t_len, D_MODEL) @ w_b_o[bi]
        x = x + mix
        x = x + 0.5 * swiglu(rms_norm(x, norm_ffn2[li]), w_ffn_in[li, 1], w_ffn_out[li, 1])
    keep = min(WINDOW, kv_k.shape[1])
    return x, jnp.stack(a_ks), jnp.stack(a_vs), kv_k[:, -keep:], kv_v[:, -keep:]


def setup_inputs(seed: int = 0) -> dict:
    key = jax.random.key(seed)
    ks = jax.random.split(key, 32)
    f32 = jnp.float32

    def nrm(k, shape, scale):
        return jax.random.normal(k, shape, f32) * scale

    def gain(k, shape):
        return 1.0 + nrm(k, shape, 0.05)

    n_pages = PAST_LEN // PAGE_SIZE
    n_used = DEC_BATCH * n_pages
    n_pool = n_used + n_used // 4
    buf_len = min(WINDOW, PAST_LEN)
    page_table = jax.random.permutation(ks[0], n_pool)[:n_used].reshape(DEC_BATCH, n_pages).astype(jnp.int32)
    d = D_MODEL
    return {
        "x_prompt": nrm(ks[1], (BATCH, SEQ, d), 1.0),
        "x_sample": nrm(ks[2], (DEC_BATCH, DEC_SEQ, d), 1.0),
        "cache_a_k": nrm(ks[3], (N_A, n_pool, PAGE_SIZE, N_HEADS, 2 * DQK), 1.0),
        "cache_a_v": nrm(ks[4], (N_A, n_pool, PAGE_SIZE, N_HEADS, DV), 1.0),
        "cache_b_k": nrm(ks[5], (DEC_BATCH, buf_len, KV_HEADS_B, HD_B), 1.0),
        "cache_b_v": nrm(ks[6], (DEC_BATCH, buf_len, KV_HEADS_B, HD_B), 1.0),
        "page_table": page_table,
        "rel_bias_table": nrm(ks[7], (N_BUCKETS, N_HEADS), 0.5),
        "norm_ffn1": gain(ks[8], (DEPTH, d)),
        "norm_attn": gain(ks[9], (DEPTH, d)),
        "norm_ffn2": gain(ks[10], (DEPTH, d)),
        "w_ffn_in": nrm(ks[11], (DEPTH, 2, d, 2 * D_FF), d ** -0.5),
        "w_ffn_out": nrm(ks[12], (DEPTH, 2, D_FF, d), D_FF ** -0.5),
        "w_a_qkv": nrm(ks[13], (N_A, d, N_HEADS * (4 * DQK + DV)), d ** -0.5),
        "a_q_norm": gain(ks[14], (N_A, DQK)),
        "a_k_norm": gain(ks[15], (N_A, DQK)),
        "a_lambda": nrm(ks[16], (N_A, 4, DQK), 0.1),
        "a_subln": gain(ks[17], (N_A, DV)),
        "w_a_o": nrm(ks[18], (N_A, N_HEADS * DV, d), (N_HEADS * DV) ** -0.5),
        "norm_kv": gain(ks[19], (d,)),
        "w_b_kv": nrm(ks[20], (d, 2 * KV_HEADS_B * HD_B), d ** -0.5),
        "b_k_norm": gain(ks[21], (HD_B,)),
        "w_b_q": nrm(ks[22], (N_B, d, N_HEADS * HD_B), d ** -0.5),
        "b_q_norm": gain(ks[23], (N_B, HD_B)),
        "b_sinks": nrm(ks[24], (N_B, N_HEADS), 0.5),
        "w_b_o": nrm(ks[25], (N_B, N_HEADS * HD_B, d), (N_HEADS * HD_B) ** -0.5),
    }


def reference(x_prompt, x_sample, cache_a_k, cache_a_v, cache_b_k, cache_b_v, page_table, rel_bias_table,
              norm_ffn1, norm_attn, norm_ffn2, w_ffn_in, w_ffn_out, w_a_qkv, a_q_norm, a_k_norm,
              a_lambda, a_subln, w_a_o, norm_kv, w_b_kv, b_k_norm, w_b_q, b_q_norm, b_sinks, w_b_o):
    y_p, ak_p, av_p, bk_p, bv_p = trunk(
        x_prompt, False, None, None, None, None, None, rel_bias_table,
        norm_ffn1, norm_attn, norm_ffn2, w_ffn_in, w_ffn_out, w_a_qkv, a_q_norm, a_k_norm,
        a_lambda, a_subln, w_a_o, norm_kv, w_b_kv, b_k_norm, w_b_q, b_q_norm, b_sinks, w_b_o)
    y_s, ak_s, av_s, bk_s, bv_s = trunk(
        x_sample, True, cache_a_k, cache_a_v, cache_b_k, cache_b_v, page_table, rel_bias_table,
        norm_ffn1, norm_attn, norm_ffn2, w_ffn_in, w_ffn_out, w_a_qkv, a_q_norm, a_k_norm,
        a_lambda, a_subln, w_a_o, norm_kv, w_b_kv, b_k_norm, w_b_q, b_q_norm, b_sinks, w_b_o)
    return (y_p, y_s, ak_p, av_p, ak_s, av_s, bk_p, bv_p, bk_s, bv_s)
```

```python
import functools
import math

import numpy as np
import jax
import jax.numpy as jnp
from jax import lax
from jax.experimental import pallas as pl
from jax.experimental.pallas import tpu as pltpu

F32 = jnp.float32
BF16 = jnp.bfloat16

EPS = 1e-6
NEG = -1e30
N_BUCKETS = 32
MAX_DISTANCE = 128
PAGE_SIZE = 128
WINDOW = 128
LANES = 128
VMEM_LIMIT = 56 * 1024 * 1024

TQ_A = 256
TK_A = 256
PAGES_PER_STEP = 4

_NT = (((1,), (1,)), ((), ()))


def _params(sem):
    return pltpu.CompilerParams(dimension_semantics=sem, vmem_limit_bytes=VMEM_LIMIT)


def _bucket_np(dist):
    n = np.maximum(dist, 0)
    max_exact = N_BUCKETS // 2
    nf = np.maximum(n, 1).astype(np.float32)
    large = max_exact + (np.log(nf / np.float32(max_exact))
                         / np.float32(math.log(MAX_DISTANCE / max_exact))
                         * np.float32(N_BUCKETS - max_exact)).astype(np.int32)
    return np.where(n < max_exact, n, np.minimum(large, N_BUCKETS - 1)).astype(np.int32)


def _bias_kernel(tbl_ref, idx_ref, o_ref):
    h = pl.program_id(0)
    idx = idx_ref[...]
    acc = jnp.zeros(idx.shape, F32)
    for b in range(N_BUCKETS):
        acc = jnp.where(idx == b, tbl_ref[b, h], acc)
    o_ref[0] = acc


def _bias_tiles(table, idx_np):
    n_heads = table.shape[1]
    r, c = idx_np.shape
    return pl.pallas_call(
        _bias_kernel,
        grid=(n_heads,),
        in_specs=[pl.BlockSpec(memory_space=pltpu.SMEM),
                  pl.BlockSpec((r, c), lambda h: (0, 0))],
        out_specs=pl.BlockSpec((1, r, c), lambda h: (h, 0, 0)),
        out_shape=jax.ShapeDtypeStruct((n_heads, r, c), F32),
        compiler_params=_params(("arbitrary",)),
        name="bias_tiles",
    )(table, jnp.asarray(idx_np))


def _lam_kernel(al_ref, o_ref, *, lam_init):
    a = al_ref[...]
    s1 = jnp.sum(a[0:1] * a[1:2], axis=-1, keepdims=True)
    s2 = jnp.sum(a[2:3] * a[3:4], axis=-1, keepdims=True)
    lam = jnp.exp(s1) - jnp.exp(s2) + lam_init
    o_ref[...] = jnp.broadcast_to(lam, o_ref.shape)


def _diff_lambda(lam_params, lam_init):
    out = pl.pallas_call(
        functools.partial(_lam_kernel, lam_init=lam_init),
        out_shape=jax.ShapeDtypeStruct((8, LANES), F32),
        name="diff_lambda",
    )(lam_params)
    return out[0, :1]


def _ffn_kernel(x_ref, g_ref, wa_ref, wb_ref, wo_ref, o_ref, xn_ref, acc_ref):
    f = pl.program_id(1)

    @pl.when(f == 0)
    def _():
        xf = x_ref[...]
        y = xf * lax.rsqrt(jnp.mean(xf * xf, axis=-1, keepdims=True) + EPS) * g_ref[...]
        xn_ref[...] = y.astype(BF16)
        acc_ref[...] = jnp.zeros_like(acc_ref)

    xn = xn_ref[...]
    a = jnp.dot(xn, wa_ref[...], preferred_element_type=F32)
    b = jnp.dot(xn, wb_ref[...], preferred_element_type=F32)
    hid = (a * (1.0 / (1.0 + jnp.exp(-a))) * b).astype(BF16)
    acc_ref[...] += jnp.dot(hid, wo_ref[...], preferred_element_type=F32)

    @pl.when(f == pl.num_programs(1) - 1)
    def _():
        o_ref[...] = x_ref[...] + 0.5 * acc_ref[...]


def _ffn(x, g, w_in, w_out, li, k, *, tm, tf):
    m, d = x.shape
    d_ff = w_out.shape[2]
    nf = d_ff // tf
    return pl.pallas_call(
        _ffn_kernel,
        grid=(m // tm, nf),
        in_specs=[pl.BlockSpec((tm, d), lambda i, f: (i, 0)),
                  pl.BlockSpec((1, d), lambda i, f: (0, 0)),
                  pl.BlockSpec((None, None, d, tf), lambda i, f: (li, k, 0, f)),
                  pl.BlockSpec((None, None, d, tf), lambda i, f: (li, k, 0, f + nf)),
                  pl.BlockSpec((None, None, tf, d), lambda i, f: (li, k, f, 0))],
        out_specs=pl.BlockSpec((tm, d), lambda i, f: (i, 0)),
        out_shape=jax.ShapeDtypeStruct((m, d), F32),
        scratch_shapes=[pltpu.VMEM((tm, d), BF16), pltpu.VMEM((tm, d), F32)],
        compiler_params=_params(("parallel", "arbitrary")),
        name="ffn",
    )(x, g.reshape(1, d), w_in, w_in, w_out)


def _proj_kernel(*refs, rms, seg, scale, residual, n_out):
    refs = list(refs)
    x_ref = refs.pop(0)
    g_ref = refs.pop(0) if rms else None
    w_ref = refs.pop(0)
    res_ref = refs.pop(0) if residual else None
    hg_ref = refs.pop(0) if seg else None
    o_refs = [refs.pop(0) for _ in range(n_out)]
    xn_ref = refs.pop(0)

    @pl.when(pl.program_id(1) == 0)
    def _():
        xf = x_ref[...].astype(F32)
        if rms:
            xf = xf * lax.rsqrt(jnp.mean(xf * xf, axis=-1, keepdims=True) + EPS) * g_ref[...]
        xn_ref[...] = xf.astype(BF16)

    acc = jnp.dot(xn_ref[...], w_ref[...], preferred_element_type=F32)
    tm, tn = acc.shape
    if residual:
        acc = res_ref[...] + acc
    if not seg:
        for o_ref in o_refs:
            o_ref[...] = acc.astype(o_ref.dtype)
        return
    lane = lax.broadcasted_iota(jnp.int32, (tm, LANES), 1)
    low = lane < seg
    gain = hg_ref[...] * scale
    for cb in range(tn // LANES):
        blk = acc[:, cb * LANES:(cb + 1) * LANES]
        sq = blk * blk
        if seg == LANES:
            ms = jnp.mean(sq, axis=-1, keepdims=True)
        else:
            s_lo = jnp.sum(jnp.where(low, sq, 0.0), axis=-1, keepdims=True)
            s_hi = jnp.sum(jnp.where(low, 0.0, sq), axis=-1, keepdims=True)
            ms = jnp.where(low, s_lo, s_hi) * (1.0 / seg)
        y = blk * lax.rsqrt(ms + EPS) * gain
        for o_ref in o_refs:
            o_ref[:, cb * LANES:(cb + 1) * LANES] = y.astype(o_ref.dtype)


def _proj(x, w, *, n, col_off=0, g=None, head_gain=None, seg=0, scale=1.0, res=None,
          out_dtypes=(F32,), tm=512, tn=512):
    m, d = x.shape
    tm = min(tm, m)
    off = col_off // tn
    ins = [x]
    specs = [pl.BlockSpec((tm, d), lambda i, j: (i, 0))]
    if g is not None:
        ins.append(g.reshape(1, d))
        specs.append(pl.BlockSpec((1, d), lambda i, j: (0, 0)))
    ins.append(w)
    specs.append(pl.BlockSpec((d, tn), lambda i, j: (0, j + off)))
    if res is not None:
        ins.append(res)
        specs.append(pl.BlockSpec((tm, tn), lambda i, j: (i, j)))
    if seg:
        hg = jnp.tile(head_gain.astype(F32), LANES // seg).reshape(1, LANES)
        ins.append(hg)
        specs.append(pl.BlockSpec((1, LANES), lambda i, j: (0, 0)))
    outs = pl.pallas_call(
        functools.partial(_proj_kernel, rms=g is not None, seg=seg, scale=scale,
                          residual=res is not None, n_out=len(out_dtypes)),
        grid=(m // tm, n // tn),
        in_specs=specs,
        out_specs=[pl.BlockSpec((tm, tn), lambda i, j: (i, j)) for _ in out_dtypes],
        out_shape=[jax.ShapeDtypeStruct((m, n), dt) for dt in out_dtypes],
        scratch_shapes=[pltpu.VMEM((tm, d), BF16)],
        compiler_params=_params(("parallel", "arbitrary")),
        name="proj",
    )(*ins)
    return outs


def _attn_a_prompt_kernel(bf_ref, lam_ref, q_ref, k_ref, v_ref, bd_ref, bn_ref, sg_ref, o_ref,
                          *, tq, tk, dqk, post_scale):
    h = pl.program_id(1)
    i = pl.program_id(2)
    q = q_ref[0]
    lane = lax.broadcasted_iota(jnp.int32, q.shape, 1)
    zero = jnp.zeros_like(q)
    qq = jnp.concatenate([jnp.where(lane < dqk, q, zero), jnp.where(lane < dqk, zero, q)], axis=0)

    def update(carry, kb, fix):
        m, l, acc = carry
        start = pl.multiple_of(kb * tk, tk)
        kblk = k_ref[0, pl.ds(start, tk), :]
        vblk = v_ref[0, pl.ds(start, tk), :]
        s = fix(lax.dot_general(qq, kblk, _NT, preferred_element_type=F32))
        m_new = jnp.maximum(m, jnp.max(s, axis=-1, keepdims=True))
        p = jnp.exp(s - m_new)
        alpha = jnp.exp(m - m_new)
        l = alpha * l + jnp.sum(p, axis=-1, keepdims=True)
        acc = alpha * acc + jnp.dot(p.astype(BF16), vblk, preferred_element_type=F32)
        return m_new, l, acc

    def fix_diag(s):
        row = lax.broadcasted_iota(jnp.int32, s.shape, 0) & (tq - 1)
        col = lax.broadcasted_iota(jnp.int32, s.shape, 1)
        b = bd_ref[0]
        return jnp.where(col <= row, s + jnp.concatenate([b, b], axis=0), NEG)

    def fix_near(s):
        b = bn_ref[0]
        return jnp.where(i >= 1, s + jnp.concatenate([b, b], axis=0), NEG)

    bfar = bf_ref[h]
    carry = (jnp.full((2 * tq, 1), NEG, F32), jnp.zeros((2 * tq, 1), F32),
             jnp.zeros((2 * tq, LANES), F32))
    carry = update(carry, i, fix_diag)
    carry = update(carry, jnp.maximum(i - 1, 0), fix_near)
    carry = lax.fori_loop(0, jnp.maximum(i - 1, 0),
                          lambda kb, c: update(c, kb, lambda s: s + bfar), carry)
    _, l, acc = carry
    o = acc / l
    d = o[:tq] - lam_ref[0] * o[tq:]
    y = d * lax.rsqrt(jnp.mean(d * d, axis=-1, keepdims=True) + EPS) * (sg_ref[...] * post_scale)
    o_ref[0] = y.astype(o_ref.dtype)


def _attn_a_prompt(q, k, v, bias_far, lam, bias_diag, bias_near, subln, post_scale, n_heads):
    bsz, s_len, _ = q.shape
    tq, tk = TQ_A, TK_A
    return pl.pallas_call(
        functools.partial(_attn_a_prompt_kernel, tq=tq, tk=tk, dqk=LANES // 2,
                          post_scale=post_scale),
        grid=(bsz, n_heads, s_len // tq),
        in_specs=[pl.BlockSpec(memory_space=pltpu.SMEM),
                  pl.BlockSpec(memory_space=pltpu.SMEM),
                  pl.BlockSpec((1, tq, LANES), lambda b, h, i: (b, i, h)),
                  pl.BlockSpec((1, s_len, LANES), lambda b, h, i: (b, 0, h)),
                  pl.BlockSpec((1, s_len, LANES), lambda b, h, i: (b, 0, h)),
                  pl.BlockSpec((1, tq, tk), lambda b, h, i: (h, 0, 0)),
                  pl.BlockSpec((1, tq, tk), lambda b, h, i: (h, 0, 0)),
                  pl.BlockSpec((1, LANES), lambda b, h, i: (0, 0))],
        out_specs=pl.BlockSpec((1, tq, LANES), lambda b, h, i: (b, i, h)),
        out_shape=jax.ShapeDtypeStruct(q.shape, BF16),
        compiler_params=_params(("parallel", "parallel", "arbitrary")),
        name="attn_a_prompt",
    )(bias_far, lam, q, k, v, bias_diag, bias_near, subln.reshape(1, LANES))


def _attn_a_sample_kernel(pt_ref, lam_ref, q_ref, *rest, n_pg, n_heads, t_len, post_scale):
    del pt_ref
    kc = rest[:n_pg]
    vc = rest[n_pg:2 * n_pg]
    (kn_ref, vn_ref, bfar_ref, blast_ref, bnew_ref, sg_ref, o_ref,
     m_ref, l_ref, acc_ref) = rest[2 * n_pg:]
    j = pl.program_id(1)
    n_steps = pl.num_programs(1)
    rows = 2 * t_len

    @pl.when(j == 0)
    def _():
        m_ref[...] = jnp.full(m_ref.shape, NEG, F32)
        l_ref[...] = jnp.zeros(l_ref.shape, F32)
        acc_ref[...] = jnp.zeros(acc_ref.shape, F32)

    def process(k_ref, v_ref, bias):
        s = lax.dot_general(q_ref[0], k_ref[0].astype(BF16), _NT,
                            preferred_element_type=F32) + bias
        m_prev = m_ref[...]
        m_new = jnp.maximum(m_prev, jnp.max(s, axis=-1, keepdims=True))
        p = jnp.exp(s - m_new)
        alpha = jnp.exp(m_prev - m_new)
        l_ref[...] = alpha * l_ref[...] + jnp.sum(p, axis=-1, keepdims=True)
        m_ref[...] = m_new
        acc_ref[...] = alpha * acc_ref[...] + jnp.dot(
            p.astype(BF16), v_ref[0].astype(BF16), preferred_element_type=F32)

    @pl.when(j < n_steps - 1)
    def _():
        for pi in range(n_pg - 1):
            process(kc[pi], vc[pi], bfar_ref[...])
        process(kc[n_pg - 1], vc[n_pg - 1], blast_ref[0])

    @pl.when(j == n_steps - 1)
    def _():
        process(kn_ref, vn_ref, bnew_ref[...])
        o = acc_ref[...] / l_ref[...]
        gain = sg_ref[...] * post_scale
        for h in range(n_heads):
            d = o[h * rows:h * rows + t_len] - lam_ref[0] * o[h * rows + t_len:(h + 1) * rows]
            y = d * lax.rsqrt(jnp.mean(d * d, axis=-1, keepdims=True) + EPS) * gain
            o_ref[0, :, h * LANES:(h + 1) * LANES] = y


def _attn_a_sample(q_rows, cache_k, cache_v, k_new, v_new, page_table, page_off, lam,
                   bias_far, bias_last, bias_new, subln, post_scale, n_heads, t_len):
    bsz, n_pages = page_table.shape
    n_pg = PAGES_PER_STEP
    n_steps = n_pages // n_pg + 1
    rows, hd = q_rows.shape[1:]
    page_rows = cache_k.shape[1]
    new_rows = k_new.shape[1]

    def page_map(p):
        def index(b, j, pt):
            return (pt[b, jnp.minimum(j, n_steps - 2) * n_pg + p] + page_off, 0, 0)
        return index

    page_specs = [pl.BlockSpec((1, page_rows, hd), page_map(p)) for p in range(n_pg)]
    new_spec = pl.BlockSpec((1, new_rows, hd), lambda b, j, pt: (b, 0, 0))
    last_spec = pl.BlockSpec((1, rows, page_rows),
                             lambda b, j, pt: (jnp.where(j >= n_steps - 2, 1, 0), 0, 0))
    grid_spec = pltpu.PrefetchScalarGridSpec(
        num_scalar_prefetch=1,
        grid=(bsz, n_steps),
        in_specs=[pl.BlockSpec(memory_space=pltpu.SMEM),
                  pl.BlockSpec((1, rows, hd), lambda b, j, pt: (b, 0, 0))]
                 + page_specs + page_specs
                 + [new_spec, new_spec,
                    pl.BlockSpec((rows, page_rows), lambda b, j, pt: (0, 0)),
                    last_spec,
                    pl.BlockSpec((rows, new_rows), lambda b, j, pt: (0, 0)),
                    pl.BlockSpec((1, LANES), lambda b, j, pt: (0, 0))],
        out_specs=pl.BlockSpec((1, t_len, n_heads * hd), lambda b, j, pt: (b, 0, 0)),
        scratch_shapes=[pltpu.VMEM((rows, 1), F32), pltpu.VMEM((rows, 1), F32),
                        pltpu.VMEM((rows, hd), F32)],
    )
    return pl.pallas_call(
        functools.partial(_attn_a_sample_kernel, n_pg=n_pg, n_heads=n_heads, t_len=t_len,
                          post_scale=post_scale),
        grid_spec=grid_spec,
        out_shape=jax.ShapeDtypeStruct((bsz, t_len, n_heads * hd), F32),
        compiler_params=_params(("parallel", "arbitrary")),
        name="attn_a_sample",
    )(page_table, lam, q_rows, *([cache_k] * n_pg), *([cache_v] * n_pg), k_new, v_new,
      bias_far, jnp.stack([bias_far, bias_last]), bias_new, subln.reshape(1, LANES))


def _sink_softmax(s, sink_col):
    m = jnp.maximum(jnp.max(s, axis=-1, keepdims=True), sink_col)
    e = jnp.exp(s - m)
    denom = jnp.sum(e, axis=-1, keepdims=True) + jnp.exp(sink_col - m)
    return (e / denom).astype(BF16)


def _swa_prompt_kernel(sink_ref, q_ref, kp_ref, kc_ref, vp_ref, vc_ref, bias_ref, o_ref, *, group):
    n = pl.program_id(1)
    kv = pl.program_id(2)
    w = WINDOW
    q = jnp.concatenate([q_ref[0, :, g * LANES:(g + 1) * LANES] for g in range(group)], axis=0)
    kcat = jnp.concatenate([kp_ref[0], kc_ref[0]], axis=0)
    vcat = jnp.concatenate([vp_ref[0], vc_ref[0]], axis=0)
    s = lax.dot_general(q, kcat, _NT, preferred_element_type=F32)
    s = s + bias_ref[...].reshape(group * w, 2 * w)
    row = lax.broadcasted_iota(jnp.int32, s.shape, 0) & (w - 1)
    col = lax.broadcasted_iota(jnp.int32, s.shape, 1)
    dist = w + row - col
    first_key = jnp.where(n > 0, 0, w)
    valid = (dist >= 0) & (dist < w) & (col >= first_key)
    s = jnp.where(valid, s, NEG)
    sink_col = jnp.concatenate(
        [jnp.full((w, 1), sink_ref[kv * group + g], F32) for g in range(group)], axis=0)
    o = jnp.dot(_sink_softmax(s, sink_col), vcat, preferred_element_type=F32)
    for g in range(group):
        o_ref[0, :, g * LANES:(g + 1) * LANES] = o[g * w:(g + 1) * w].astype(o_ref.dtype)


def _swa_prompt(q, k, v, sinks, bias, kv_heads, group):
    bsz, s_len, _ = q.shape
    nb = s_len // WINDOW
    gw = group * LANES
    prev = lambda b, n, kv: (b, jnp.maximum(n - 1, 0), kv)
    cur = lambda b, n, kv: (b, n, kv)
    return pl.pallas_call(
        functools.partial(_swa_prompt_kernel, group=group),
        grid=(bsz, nb, kv_heads),
        in_specs=[pl.BlockSpec(memory_space=pltpu.SMEM),
                  pl.BlockSpec((1, WINDOW, gw), cur),
                  pl.BlockSpec((1, WINDOW, LANES), prev),
                  pl.BlockSpec((1, WINDOW, LANES), cur),
                  pl.BlockSpec((1, WINDOW, LANES), prev),
                  pl.BlockSpec((1, WINDOW, LANES), cur),
                  pl.BlockSpec((group, WINDOW, 2 * WINDOW), lambda b, n, kv: (kv, 0, 0))],
        out_specs=pl.BlockSpec((1, WINDOW, gw), cur),
        out_shape=jax.ShapeDtypeStruct(q.shape, BF16),
        compiler_params=_params(("parallel", "parallel", "arbitrary")),
        name="swa_prompt",
    )(sinks, q, k, k, v, v, bias)


def _swa_sample_kernel(sink_ref, q_ref, k_ref, v_ref, bias_ref, o_ref, *, kv_heads, group, t_len):
    w = WINDOW
    for kv in range(kv_heads):
        q = jnp.concatenate(
            [q_ref[0, :, (kv * group + g) * LANES:(kv * group + g + 1) * LANES]
             for g in range(group)], axis=0).astype(BF16)
        kk = k_ref[0, :, kv * LANES:(kv + 1) * LANES]
        vv = v_ref[0, :, kv * LANES:(kv + 1) * LANES]
        s = lax.dot_general(q, kk, _NT, preferred_element_type=F32)
        s = s + bias_ref[kv * group:(kv + 1) * group].reshape(group * t_len, 2 * w)
        row = lax.broadcasted_iota(jnp.int32, s.shape, 0) & (t_len - 1)
        col = lax.broadcasted_iota(jnp.int32, s.shape, 1)
        dist = w + row - col
        s = jnp.where((dist >= 0) & (dist < w), s, NEG)
        sink_col = jnp.concatenate(
            [jnp.full((t_len, 1), sink_ref[kv * group + g], F32) for g in range(group)], axis=0)
        o = jnp.dot(_sink_softmax(s, sink_col), vv, preferred_element_type=F32)
        for g in range(group):
            hh = kv * group + g
            o_ref[0, :, hh * LANES:(hh + 1) * LANES] = o[g * t_len:(g + 1) * t_len]


def _swa_sample(q, kk, vv, sinks, bias, kv_heads, group):
    bsz, t_len, d = q.shape
    n_keys = kk.shape[1]
    return pl.pallas_call(
        functools.partial(_swa_sample_kernel, kv_heads=kv_heads, group=group, t_len=t_len),
        grid=(bsz,),
        in_specs=[pl.BlockSpec(memory_space=pltpu.SMEM),
                  pl.BlockSpec((1, t_len, d), lambda b: (b, 0, 0)),
                  pl.BlockSpec((1, n_keys, kv_heads * LANES), lambda b: (b, 0, 0)),
                  pl.BlockSpec((1, n_keys, kv_heads * LANES), lambda b: (b, 0, 0)),
                  pl.BlockSpec(bias.shape, lambda b: (0, 0, 0))],
        out_specs=pl.BlockSpec((1, t_len, d), lambda b: (b, 0, 0)),
        out_shape=jax.ShapeDtypeStruct(q.shape, F32),
        compiler_params=_params(("parallel",)),
        name="swa_sample",
    )(sinks, q, kk, vv, bias)


def _sample_bias_tiles(swa_t, far, n_heads, t_len):
    rows = n_heads * 2 * t_len
    same = jnp.eye(n_heads, dtype=bool)[:, None, None, None, :]

    def tile(per_key, keep):
        b = jnp.where(keep[None], per_key, NEG)[:, None, :, :, None]
        n_keys = per_key.shape[-1]
        full = jnp.where(same, b, NEG)
        full = jnp.broadcast_to(full, (n_heads, 2, t_len, n_keys, n_heads))
        return full.reshape(rows, n_keys * n_heads)

    all_keys = jnp.ones((t_len, PAGE_SIZE), bool)
    far_tile = tile(jnp.broadcast_to(far[:, None, None], (n_heads, t_len, PAGE_SIZE)), all_keys)
    last_tile = tile(swa_t[:, :, :PAGE_SIZE], all_keys)
    causal = jnp.arange(t_len)[None, :] <= jnp.arange(t_len)[:, None]
    new_tile = tile(swa_t[:, :, PAGE_SIZE:PAGE_SIZE + t_len], causal)
    return far_tile, last_tile, new_tile


def _query_rows(q, n_heads, dqk):
    bsz, t_len, _ = q.shape
    qh = jnp.transpose(q.reshape(bsz, t_len, n_heads, 2 * dqk), (0, 2, 1, 3))
    comp = (jnp.arange(2 * dqk)[None, :] // dqk) == jnp.arange(2)[:, None]
    rows = jnp.where(comp[None, None, :, None, :], qh[:, :, None], 0.0)
    return rows.reshape(bsz, n_heads * 2 * t_len, 2 * dqk)


def _trunk(x, sample, caches, bias, weights, dims):
    (cache_a_k, cache_a_v, cache_b_k, cache_b_v, page_table) = caches
    (norm_ffn1, norm_attn, norm_ffn2, w_ffn_in, w_ffn_out, w_a_qkv, a_q_norm, a_k_norm,
     a_lambda, a_subln, w_a_o, norm_kv, w_b_kv, b_k_norm, w_b_q, b_q_norm, b_sinks, w_b_o) = weights
    depth, n_a, n_heads, kv_heads = dims
    bsz, t_len, d = x.shape
    m = bsz * t_len
    group = n_heads // kv_heads
    dqk = d // n_heads // 2
    hd = d // n_heads
    kv_w = kv_heads * hd
    tm = min(512, m)
    x = x.reshape(m, d)
    a_ks, a_vs = [], []
    kv_k = kv_v = kb16 = vb16 = None
    for li in range(depth):
        if li == n_a:
            kv_k, kb16 = _proj(x, w_b_kv, n=kv_w, g=norm_kv, head_gain=b_k_norm, seg=hd,
                               out_dtypes=(F32, BF16), tm=tm)
            kv_v, vb16 = _proj(x, w_b_kv, n=kv_w, col_off=kv_w, g=norm_kv,
                               out_dtypes=(F32, BF16), tm=tm)
        x = _ffn(x, norm_ffn1[li], w_ffn_in, w_ffn_out, li, 0, tm=tm, tf=512)
        q_dt = F32 if sample else BF16
        if li < n_a:
            lam_init = 0.8 - 0.6 * math.exp(-0.3 * li)
            lam = _diff_lambda(a_lambda[li], lam_init)
            (q,) = _proj(x, w_a_qkv[li], n=d, g=norm_attn[li], head_gain=a_q_norm[li], seg=dqk,
                         scale=dqk ** -0.5, out_dtypes=(q_dt,), tm=tm)
            k, k16 = _proj(x, w_a_qkv[li], n=d, col_off=d, g=norm_attn[li],
                           head_gain=a_k_norm[li], seg=dqk, out_dtypes=(F32, BF16), tm=tm)
            v, v16 = _proj(x, w_a_qkv[li], n=d, col_off=2 * d, g=norm_attn[li],
                           out_dtypes=(F32, BF16), tm=tm)
            a_ks.append(k.reshape(bsz, t_len, n_heads, hd))
            a_vs.append(v.reshape(bsz, t_len, n_heads, hd))
            if sample:
                n_pool = cache_a_k.shape[1]
                q_rows = _query_rows(q.reshape(bsz, t_len, d), n_heads, dqk).astype(BF16)
                o = _attn_a_sample(
                    q_rows,
                    cache_a_k.reshape(-1, PAGE_SIZE * n_heads, hd),
                    cache_a_v.reshape(-1, PAGE_SIZE * n_heads, hd),
                    k.reshape(bsz, t_len * n_heads, hd), v.reshape(bsz, t_len * n_heads, hd),
                    page_table, li * n_pool, lam, bias["a_far"], bias["a_last"], bias["a_new"],
                    a_subln[li], 1.0 - lam_init, n_heads, t_len)
            else:
                o = _attn_a_prompt(
                    q.reshape(bsz, t_len, d), k16.reshape(bsz, t_len, d), v16.reshape(bsz, t_len, d),
                    bias["far"], lam, bias["diag"], bias["near"], a_subln[li], 1.0 - lam_init,
                    n_heads)
            (x,) = _proj(o.reshape(m, d), w_a_o[li], n=d, res=x, tm=tm)
        else:
            bi = li - n_a
            (q,) = _proj(x, w_b_q[bi], n=d, g=norm_attn[li], head_gain=b_q_norm[bi], seg=hd,
                         scale=hd ** -0.5, out_dtypes=(q_dt,), tm=tm)
            if sample:
                zpad = jnp.zeros((bsz, WINDOW - t_len, kv_w), BF16)
                kk = jnp.concatenate([cache_b_k.reshape(bsz, WINDOW, kv_w).astype(BF16),
                                      kb16.reshape(bsz, t_len, kv_w), zpad], axis=1)
                vv = jnp.concatenate([cache_b_v.reshape(bsz, WINDOW, kv_w).astype(BF16),
                                      vb16.reshape(bsz, t_len, kv_w), zpad], axis=1)
                o = _swa_sample(q.reshape(bsz, t_len, d), kk, vv, b_sinks[bi], bias["swa_s"],
                                kv_heads, group)
            else:
                o = _swa_prompt(q.reshape(bsz, t_len, d), kb16.reshape(bsz, t_len, kv_w),
                                vb16.reshape(bsz, t_len, kv_w), b_sinks[bi], bias["swa_p"],
                                kv_heads, group)
            (x,) = _proj(o.reshape(m, d), w_b_o[bi], n=d, res=x, tm=tm)
        x = _ffn(x, norm_ffn2[li], w_ffn_in, w_ffn_out, li, 1, tm=tm, tf=512)
    kv_k = kv_k.reshape(bsz, t_len, kv_heads, hd)
    kv_v = kv_v.reshape(bsz, t_len, kv_heads, hd)
    if sample:
        kv_k = jnp.concatenate([cache_b_k, kv_k], axis=1)
        kv_v = jnp.concatenate([cache_b_v, kv_v], axis=1)
    keep = min(WINDOW, kv_k.shape[1])
    return (x.reshape(bsz, t_len, d), jnp.stack(a_ks), jnp.stack(a_vs),
            kv_k[:, -keep:], kv_v[:, -keep:])


def kernel(x_prompt, x_sample, cache_a_k, cache_a_v, cache_b_k, cache_b_v, page_table, rel_bias_table, norm_ffn1, norm_attn, norm_ffn2, w_ffn_in, w_ffn_out, w_a_qkv, a_q_norm, a_k_norm, a_lambda, a_subln, w_a_o, norm_kv, w_b_kv, b_k_norm, w_b_q, b_q_norm, b_sinks, w_b_o):
    depth = norm_ffn1.shape[0]
    n_a = w_a_qkv.shape[0]
    n_heads = rel_bias_table.shape[1]
    kv_heads = cache_b_k.shape[2]
    t_len = x_sample.shape[1]
    dims = (depth, n_a, n_heads, kv_heads)

    r = np.arange(TQ_A)[:, None]
    c = np.arange(TK_A)[None, :]
    diag = _bias_tiles(rel_bias_table, _bucket_np(r - c))
    near = _bias_tiles(rel_bias_table, _bucket_np(TK_A + r - c))
    rw = np.arange(WINDOW)[:, None]
    cw = np.arange(2 * WINDOW)[None, :]
    swa = _bias_tiles(rel_bias_table, _bucket_np(WINDOW + rw - cw))
    far = rel_bias_table[N_BUCKETS - 1]
    swa_t = swa[:, :t_len]
    a_far, a_last, a_new = _sample_bias_tiles(swa_t, far, n_heads, t_len)
    bias = {"far": far, "diag": diag, "near": near, "swa_p": swa, "swa_s": swa_t,
            "a_far": a_far, "a_last": a_last, "a_new": a_new}

    wcast = lambda w: w.astype(BF16)
    weights = (norm_ffn1, norm_attn, norm_ffn2, wcast(w_ffn_in), wcast(w_ffn_out), wcast(w_a_qkv),
               a_q_norm, a_k_norm, a_lambda, a_subln, wcast(w_a_o), norm_kv, wcast(w_b_kv),
               b_k_norm, wcast(w_b_q), b_q_norm, b_sinks, wcast(w_b_o))
    none5 = (None,) * 5
    y_p, ak_p, av_p, bk_p, bv_p = _trunk(x_prompt, False, none5, bias, weights, dims)
    y_s, ak_s, av_s, bk_s, bv_s = _trunk(
        x_sample, True, (cache_a_k, cache_a_v, cache_b_k, cache_b_v, page_table), bias, weights, dims)
    return (y_p, y_s, ak_p, av_p, ak_s, av_s, bk_p, bv_p, bk_s, bv_s)
```

```python
import functools
import math

import numpy as np
import jax
import jax.numpy as jnp
from jax import lax
from jax.experimental import pallas as pl
from jax.experimental.pallas import tpu as pltpu

F32 = jnp.float32
BF16 = jnp.bfloat16

EPS = 1e-6
NEG = -1e30
LOG2E = math.log2(math.e)
N_BUCKETS = 32
MAX_DISTANCE = 128
PAGE_SIZE = 128
WINDOW = 128
LANES = 128
VMEM_LIMIT = 56 * 1024 * 1024

TQ_A = 256
HEADS_A = 2
PAGES_PER_STEP = 4

_NT = (((1,), (1,)), ((), ()))


def _params(sem):
    return pltpu.CompilerParams(dimension_semantics=sem, vmem_limit_bytes=VMEM_LIMIT)


def _bucket_np(dist):
    n = np.maximum(dist, 0)
    max_exact = N_BUCKETS // 2
    nf = np.maximum(n, 1).astype(np.float32)
    large = max_exact + (np.log(nf / np.float32(max_exact))
                         / np.float32(math.log(MAX_DISTANCE / max_exact))
                         * np.float32(N_BUCKETS - max_exact)).astype(np.int32)
    bucket = np.where(n < max_exact, n, np.minimum(large, N_BUCKETS - 1))
    return np.where(dist < 0, -1, bucket).astype(np.int32)


def _bias_kernel(tbl_ref, idx_ref, o_ref, *, rel_far, scale):
    h = pl.program_id(0)
    idx = idx_ref[...]
    acc = jnp.zeros(idx.shape, F32)
    for b in range(N_BUCKETS):
        acc = jnp.where(idx == b, tbl_ref[b, h], acc)
    if rel_far:
        acc = jnp.where(idx < 0, NEG, (acc - tbl_ref[N_BUCKETS - 1, h]) * scale)
    o_ref[0] = acc


def _bias_tiles(table, idx_np, *, rel_far=False, scale=1.0):
    n_heads = table.shape[1]
    r, c = idx_np.shape
    return pl.pallas_call(
        functools.partial(_bias_kernel, rel_far=rel_far, scale=scale),
        grid=(n_heads,),
        in_specs=[pl.BlockSpec(memory_space=pltpu.SMEM),
                  pl.BlockSpec((r, c), lambda h: (0, 0))],
        out_specs=pl.BlockSpec((1, r, c), lambda h: (h, 0, 0)),
        out_shape=jax.ShapeDtypeStruct((n_heads, r, c), F32),
        compiler_params=_params(("arbitrary",)),
        name="bias_tiles",
    )(table, jnp.asarray(idx_np))


def _lam_kernel(al_ref, o_ref, *, lam_init):
    a = al_ref[...]
    s1 = jnp.sum(a[0:1] * a[1:2], axis=-1, keepdims=True)
    s2 = jnp.sum(a[2:3] * a[3:4], axis=-1, keepdims=True)
    lam = jnp.exp(s1) - jnp.exp(s2) + lam_init
    o_ref[...] = jnp.broadcast_to(lam, o_ref.shape)


def _diff_lambda(lam_params, lam_init):
    out = pl.pallas_call(
        functools.partial(_lam_kernel, lam_init=lam_init),
        out_shape=jax.ShapeDtypeStruct((8, LANES), F32),
        name="diff_lambda",
    )(lam_params)
    return out[0, :1]


def _ffn_kernel(x_ref, g_ref, wa_ref, wb_ref, wo_ref, o_ref, xn_ref, acc_ref):
    f = pl.program_id(1)

    @pl.when(f == 0)
    def _():
        xf = x_ref[...]
        y = xf * lax.rsqrt(jnp.mean(xf * xf, axis=-1, keepdims=True) + EPS) * g_ref[...]
        xn_ref[...] = y.astype(BF16)
        acc_ref[...] = jnp.zeros_like(acc_ref)

    xn = xn_ref[...]
    a = jnp.dot(xn, wa_ref[...], preferred_element_type=F32)
    b = jnp.dot(xn, wb_ref[...], preferred_element_type=F32)
    hid = (a * (1.0 / (1.0 + jnp.exp(-a))) * b).astype(BF16)
    acc_ref[...] += jnp.dot(hid, wo_ref[...], preferred_element_type=F32)

    @pl.when(f == pl.num_programs(1) - 1)
    def _():
        o_ref[...] = x_ref[...] + 0.5 * acc_ref[...]


def _ffn(x, g, w_in, w_out, li, k, *, tm, tf):
    m, d = x.shape
    d_ff = w_out.shape[2]
    nf = d_ff // tf
    return pl.pallas_call(
        _ffn_kernel,
        grid=(m // tm, nf),
        in_specs=[pl.BlockSpec((tm, d), lambda i, f: (i, 0)),
                  pl.BlockSpec((1, d), lambda i, f: (0, 0)),
                  pl.BlockSpec((None, None, d, tf), lambda i, f: (li, k, 0, f)),
                  pl.BlockSpec((None, None, d, tf), lambda i, f: (li, k, 0, f + nf)),
                  pl.BlockSpec((None, None, tf, d), lambda i, f: (li, k, f, 0))],
        out_specs=pl.BlockSpec((tm, d), lambda i, f: (i, 0)),
        out_shape=jax.ShapeDtypeStruct((m, d), F32),
        scratch_shapes=[pltpu.VMEM((tm, d), BF16), pltpu.VMEM((tm, d), F32)],
        compiler_params=_params(("parallel", "arbitrary")),
        name="ffn",
    )(x, g.reshape(1, d), w_in, w_in, w_out)


def _proj_kernel(*refs, rms, seg, scale, residual, n_out):
    refs = list(refs)
    x_ref = refs.pop(0)
    g_ref = refs.pop(0) if rms else None
    w_ref = refs.pop(0)
    res_ref = refs.pop(0) if residual else None
    hg_ref = refs.pop(0) if seg else None
    o_refs = [refs.pop(0) for _ in range(n_out)]
    xn_ref = refs.pop(0)

    @pl.when(pl.program_id(1) == 0)
    def _():
        xf = x_ref[...].astype(F32)
        if rms:
            xf = xf * lax.rsqrt(jnp.mean(xf * xf, axis=-1, keepdims=True) + EPS) * g_ref[...]
        xn_ref[...] = xf.astype(BF16)

    acc = jnp.dot(xn_ref[...], w_ref[...], preferred_element_type=F32)
    tm, tn = acc.shape
    if residual:
        acc = res_ref[...] + acc
    if not seg:
        for o_ref in o_refs:
            o_ref[...] = acc.astype(o_ref.dtype)
        return
    lane = lax.broadcasted_iota(jnp.int32, (tm, LANES), 1)
    low = lane < seg
    gain = hg_ref[...] * scale
    for cb in range(tn // LANES):
        blk = acc[:, cb * LANES:(cb + 1) * LANES]
        sq = blk * blk
        if seg == LANES:
            ms = jnp.mean(sq, axis=-1, keepdims=True)
        else:
            s_lo = jnp.sum(jnp.where(low, sq, 0.0), axis=-1, keepdims=True)
            s_hi = jnp.sum(jnp.where(low, 0.0, sq), axis=-1, keepdims=True)
            ms = jnp.where(low, s_lo, s_hi) * (1.0 / seg)
        y = blk * lax.rsqrt(ms + EPS) * gain
        for o_ref in o_refs:
            o_ref[:, cb * LANES:(cb + 1) * LANES] = y.astype(o_ref.dtype)


def _proj(x, w, *, n, col_off=0, g=None, head_gain=None, seg=0, scale=1.0, res=None,
          out_dtypes=(F32,), tm=512, tn=512):
    m, d = x.shape
    tm = min(tm, m)
    off = col_off // tn
    ins = [x]
    specs = [pl.BlockSpec((tm, d), lambda i, j: (i, 0))]
    if g is not None:
        ins.append(g.reshape(1, d))
        specs.append(pl.BlockSpec((1, d), lambda i, j: (0, 0)))
    ins.append(w)
    specs.append(pl.BlockSpec((d, tn), lambda i, j: (0, j + off)))
    if res is not None:
        ins.append(res)
        specs.append(pl.BlockSpec((tm, tn), lambda i, j: (i, j)))
    if seg:
        hg = jnp.tile(head_gain.astype(F32), LANES // seg).reshape(1, LANES)
        ins.append(hg)
        specs.append(pl.BlockSpec((1, LANES), lambda i, j: (0, 0)))
    outs = pl.pallas_call(
        functools.partial(_proj_kernel, rms=g is not None, seg=seg, scale=scale,
                          residual=res is not None, n_out=len(out_dtypes)),
        grid=(m // tm, n // tn),
        in_specs=specs,
        out_specs=[pl.BlockSpec((tm, tn), lambda i, j: (i, j)) for _ in out_dtypes],
        out_shape=[jax.ShapeDtypeStruct((m, n), dt) for dt in out_dtypes],
        scratch_shapes=[pltpu.VMEM((tm, d), BF16)],
        compiler_params=_params(("parallel", "arbitrary")),
        name="proj",
    )(*ins)
    return outs


def _attn_a_prompt_kernel(lam_ref, q_ref, k_ref, vt_ref, bt_ref, sg_ref, o_ref,
                          m_ref, l_ref, acc_ref, *, tq, dqk, n_hd, post_scale):
    i = pl.program_id(2)
    lane = lax.broadcasted_iota(jnp.int32, (tq, LANES), 1)
    qqs = []
    for g in range(n_hd):
        q = q_ref[0, :, g * LANES:(g + 1) * LANES]
        zero = jnp.zeros_like(q)
        qqs.append(jnp.concatenate([jnp.where(lane < dqk, q, zero),
                                    jnp.where(lane < dqk, zero, q)], axis=0))
    m_ref[...] = jnp.full(m_ref.shape, NEG, F32)
    l_ref[...] = jnp.zeros(l_ref.shape, F32)
    acc_ref[...] = jnp.zeros(acc_ref.shape, F32)

    def update(start, size, biases):
        for g in range(n_hd):
            kblk = k_ref[0, pl.ds(start, size), g * LANES:(g + 1) * LANES]
            vtblk = vt_ref[0, g * LANES:(g + 1) * LANES, pl.ds(start, size)]
            s = lax.dot_general(kblk, qqs[g], _NT, preferred_element_type=F32)
            if biases is not None:
                s = s + biases[g]
            m_prev = m_ref[g]
            m_new = jnp.maximum(m_prev, jnp.max(s, axis=0, keepdims=True))
            p = jnp.exp2(s - m_new)
            alpha = jnp.exp2(m_prev - m_new)
            m_ref[g] = m_new
            l_ref[g] = alpha * l_ref[g] + jnp.sum(p, axis=0, keepdims=True)
            acc_ref[g] = alpha * acc_ref[g] + jnp.dot(vtblk, p.astype(BF16),
                                                      preferred_element_type=F32)

    tails = []
    for g in range(n_hd):
        b = bt_ref[g, 0]
        tails.append(jnp.concatenate([b, b], axis=1))
    update(pl.multiple_of(jnp.maximum(i - 1, 0) * tq, tq), 2 * tq, tails)

    def far(f, c):
        update(pl.multiple_of(f * 2 * tq, 2 * tq), 2 * tq, None)
        return c
    lax.fori_loop(0, jnp.maximum(i - 1, 0) // 2, far, 0)

    @pl.when(jnp.logical_and(i >= 2, (i & 1) == 0))
    def _():
        update(pl.multiple_of((i - 2) * tq, tq), tq, None)

    gain = sg_ref[...] * post_scale
    for g in range(n_hd):
        o = acc_ref[g] / l_ref[g]
        d = o[:, :tq] - lam_ref[0] * o[:, tq:]
        y = d * lax.rsqrt(jnp.mean(d * d, axis=0, keepdims=True) + EPS) * gain
        o_ref[0, :, g * LANES:(g + 1) * LANES] = y.T.astype(o_ref.dtype)


def _attn_a_prompt(q, k, vt, lam, bias_tail, subln, post_scale, n_heads):
    bsz, s_len, _ = q.shape
    tq, n_hd = TQ_A, HEADS_A
    gw = n_hd * LANES
    return pl.pallas_call(
        functools.partial(_attn_a_prompt_kernel, tq=tq, dqk=LANES // 2, n_hd=n_hd,
                          post_scale=post_scale),
        grid=(bsz, n_heads // n_hd, s_len // tq),
        in_specs=[pl.BlockSpec(memory_space=pltpu.SMEM),
                  pl.BlockSpec((1, tq, gw), lambda b, h, i: (b, i, h)),
                  pl.BlockSpec((1, s_len, gw), lambda b, h, i: (b, 0, h)),
                  pl.BlockSpec((1, gw, s_len), lambda b, h, i: (b, h, 0)),
                  pl.BlockSpec((n_hd, 1, 2 * tq, tq), lambda b, h, i: (h, jnp.minimum(i, 1), 0, 0)),
                  pl.BlockSpec((LANES, 1), lambda b, h, i: (0, 0))],
        out_specs=pl.BlockSpec((1, tq, gw), lambda b, h, i: (b, i, h)),
        out_shape=jax.ShapeDtypeStruct(q.shape, BF16),
        scratch_shapes=[pltpu.VMEM((n_hd, 1, 2 * tq), F32), pltpu.VMEM((n_hd, 1, 2 * tq), F32),
                        pltpu.VMEM((n_hd, LANES, 2 * tq), F32)],
        compiler_params=_params(("parallel", "parallel", "arbitrary")),
        name="attn_a_prompt",
    )(lam, q, k, vt, bias_tail, subln.reshape(LANES, 1))


def _attn_a_sample_kernel(pt_ref, lam_ref, qbd_ref, k_hbm, v_hbm, kn_ref, vn_ref, bias_ref,
                          bnew_ref, sg_ref, o_ref, kbuf, vbuf, sem, m_ref, l_ref, acc_ref,
                          *, n_pg, n_heads, t_len, page_off, post_scale):
    b = pl.program_id(0)
    j = pl.program_id(1)
    n_steps = pl.num_programs(1)
    n_total = pl.num_programs(0) * n_steps
    g = b * n_steps + j
    slot = g % 2
    rows = 2 * t_len
    cols = n_heads * rows

    def page_copies(bb, jj, sl):
        out = []
        for p in range(n_pg):
            page = pt_ref[bb, jj * n_pg + p] + page_off
            for h in range(n_heads):
                out.append(pltpu.make_async_copy(k_hbm.at[page, :, h, :], kbuf.at[sl, p, h],
                                                 sem.at[sl]))
                out.append(pltpu.make_async_copy(v_hbm.at[page, :, h, :], vbuf.at[sl, p, h],
                                                 sem.at[sl]))
        return out

    @pl.when(g == 0)
    def _():
        for c in page_copies(b, j, slot):
            c.start()

    @pl.when(g + 1 < n_total)
    def _():
        nxt = g + 1
        for c in page_copies(nxt // n_steps, nxt % n_steps, 1 - slot):
            c.start()

    @pl.when(j == 0)
    def _():
        m_ref[...] = jnp.full(m_ref.shape, NEG, F32)
        l_ref[...] = jnp.zeros(l_ref.shape, F32)
        acc_ref[...] = jnp.zeros(acc_ref.shape, F32)

    for c in page_copies(b, j, slot):
        c.wait()

    def process(k_tiles, v_tiles, bias):
        n_p = len(k_tiles)
        kk = jnp.concatenate(
            [jnp.concatenate([t().astype(BF16) for t in k_tiles[p]], axis=1) for p in range(n_p)],
            axis=0)
        s = jnp.dot(kk, qbd_ref[0], preferred_element_type=F32) + bias
        m_prev = m_ref[...]
        m_new = jnp.maximum(m_prev, jnp.max(s, axis=0, keepdims=True))
        p_t = jnp.exp2(s - m_new)
        alpha = jnp.exp2(m_prev - m_new)
        m_ref[...] = m_new
        l_ref[...] = alpha * l_ref[...] + jnp.sum(p_t, axis=0, keepdims=True)
        pb = p_t.T.astype(BF16)
        heads = []
        for h in range(n_heads):
            vh = jnp.concatenate([v_tiles[p][h]().astype(BF16) for p in range(n_p)], axis=0)
            heads.append(jnp.dot(pb[h * rows:(h + 1) * rows], vh, preferred_element_type=F32))
        alpha_col = jnp.broadcast_to(alpha, (8, cols)).T[:, :1]
        acc_ref[...] = alpha_col * acc_ref[...] + jnp.concatenate(heads, axis=0)

    k_tiles = [[(lambda p=p, h=h: kbuf[slot, p, h]) for h in range(n_heads)] for p in range(n_pg)]
    v_tiles = [[(lambda p=p, h=h: vbuf[slot, p, h]) for h in range(n_heads)] for p in range(n_pg)]
    process(k_tiles, v_tiles, bias_ref[0])

    @pl.when(j == n_steps - 1)
    def _():
        kn = [[(lambda h=h: kn_ref[0, :, h * LANES:(h + 1) * LANES]) for h in range(n_heads)]]
        vn = [[(lambda h=h: vn_ref[0, :, h * LANES:(h + 1) * LANES]) for h in range(n_heads)]]
        process(kn, vn, bnew_ref[...])
        l_col = jnp.broadcast_to(l_ref[...], (8, cols)).T[:, :1]
        o = acc_ref[...] / l_col
        gain = sg_ref[...] * post_scale
        for h in range(n_heads):
            d = o[h * rows:h * rows + t_len] - lam_ref[0] * o[h * rows + t_len:(h + 1) * rows]
            y = d * lax.rsqrt(jnp.mean(d * d, axis=-1, keepdims=True) + EPS) * gain
            o_ref[0, :, h * LANES:(h + 1) * LANES] = y


def _attn_a_sample(qbd, cache_k, cache_v, k_new, v_new, page_table, page_off, lam,
                   bias_steps, bias_new, subln, post_scale, n_heads, t_len):
    bsz, n_pages = page_table.shape
    n_pg = PAGES_PER_STEP
    n_steps = n_pages // n_pg
    d, cols = qbd.shape[1:]
    hd = d // n_heads
    grid_spec = pltpu.PrefetchScalarGridSpec(
        num_scalar_prefetch=1,
        grid=(bsz, n_steps),
        in_specs=[pl.BlockSpec(memory_space=pltpu.SMEM),
                  pl.BlockSpec((1, d, cols), lambda b, j, pt: (b, 0, 0)),
                  pl.BlockSpec(memory_space=pl.ANY),
                  pl.BlockSpec(memory_space=pl.ANY),
                  pl.BlockSpec((1, PAGE_SIZE, d), lambda b, j, pt: (b, 0, 0)),
                  pl.BlockSpec((1, PAGE_SIZE, d), lambda b, j, pt: (b, 0, 0)),
                  pl.BlockSpec((1, n_pg * PAGE_SIZE, cols),
                               lambda b, j, pt: (jnp.where(j == n_steps - 1, 1, 0), 0, 0)),
                  pl.BlockSpec((PAGE_SIZE, cols), lambda b, j, pt: (0, 0)),
                  pl.BlockSpec((1, LANES), lambda b, j, pt: (0, 0))],
        out_specs=pl.BlockSpec((1, t_len, d), lambda b, j, pt: (b, 0, 0)),
        scratch_shapes=[pltpu.VMEM((2, n_pg, n_heads, PAGE_SIZE, hd), F32),
                        pltpu.VMEM((2, n_pg, n_heads, PAGE_SIZE, hd), F32),
                        pltpu.SemaphoreType.DMA((2,)),
                        pltpu.VMEM((1, cols), F32), pltpu.VMEM((1, cols), F32),
                        pltpu.VMEM((cols, hd), F32)],
    )
    return pl.pallas_call(
        functools.partial(_attn_a_sample_kernel, n_pg=n_pg, n_heads=n_heads, t_len=t_len,
                          page_off=page_off, post_scale=post_scale),
        grid_spec=grid_spec,
        out_shape=jax.ShapeDtypeStruct((bsz, t_len, d), F32),
        compiler_params=_params(("arbitrary", "arbitrary")),
        name="attn_a_sample",
    )(page_table, lam, qbd, cache_k, cache_v, k_new, v_new, bias_steps, bias_new,
      subln.reshape(1, LANES))


def _sink_softmax(s, sink_col):
    m = jnp.maximum(jnp.max(s, axis=-1, keepdims=True), sink_col)
    e = jnp.exp(s - m)
    denom = jnp.sum(e, axis=-1, keepdims=True) + jnp.exp(sink_col - m)
    return (e / denom).astype(BF16)


def _swa_prompt_kernel(sink_ref, q_ref, kp_ref, kc_ref, vp_ref, vc_ref, bias_ref, o_ref,
                       *, kv_heads, group):
    n = pl.program_id(1)
    w = WINDOW
    row = lax.broadcasted_iota(jnp.int32, (group * w, 2 * w), 0) & (w - 1)
    col = lax.broadcasted_iota(jnp.int32, (group * w, 2 * w), 1)
    dist = w + row - col
    first_key = jnp.where(n > 0, 0, w)
    valid = (dist >= 0) & (dist < w) & (col >= first_key)
    for kv in range(kv_heads):
        hs = [kv * group + g for g in range(group)]
        q = jnp.concatenate([q_ref[0, :, h * LANES:(h + 1) * LANES] for h in hs], axis=0)
        ksl = slice(kv * LANES, (kv + 1) * LANES)
        kcat = jnp.concatenate([kp_ref[0, :, ksl], kc_ref[0, :, ksl]], axis=0)
        vcat = jnp.concatenate([vp_ref[0, :, ksl], vc_ref[0, :, ksl]], axis=0)
        s = lax.dot_general(q, kcat, _NT, preferred_element_type=F32)
        s = s + bias_ref[kv * group:(kv + 1) * group].reshape(group * w, 2 * w)
        s = jnp.where(valid, s, NEG)
        sink_col = jnp.concatenate([jnp.full((w, 1), sink_ref[h], F32) for h in hs], axis=0)
        o = jnp.dot(_sink_softmax(s, sink_col), vcat, preferred_element_type=F32)
        for g, h in enumerate(hs):
            o_ref[0, :, h * LANES:(h + 1) * LANES] = o[g * w:(g + 1) * w].astype(o_ref.dtype)


def _swa_prompt(q, k, v, sinks, bias, kv_heads, group):
    bsz, s_len, d = q.shape
    nb = s_len // WINDOW
    kw = kv_heads * LANES
    prev = lambda b, n: (b, jnp.maximum(n - 1, 0), 0)
    cur = lambda b, n: (b, n, 0)
    return pl.pallas_call(
        functools.partial(_swa_prompt_kernel, kv_heads=kv_heads, group=group),
        grid=(bsz, nb),
        in_specs=[pl.BlockSpec(memory_space=pltpu.SMEM),
                  pl.BlockSpec((1, WINDOW, d), cur),
                  pl.BlockSpec((1, WINDOW, kw), prev),
                  pl.BlockSpec((1, WINDOW, kw), cur),
                  pl.BlockSpec((1, WINDOW, kw), prev),
                  pl.BlockSpec((1, WINDOW, kw), cur),
                  pl.BlockSpec(bias.shape, lambda b, n: (0, 0, 0))],
        out_specs=pl.BlockSpec((1, WINDOW, d), cur),
        out_shape=jax.ShapeDtypeStruct(q.shape, BF16),
        compiler_params=_params(("parallel", "arbitrary")),
        name="swa_prompt",
    )(sinks, q, k, k, v, v, bias)


def _swa_sample_kernel(sink_ref, q_ref, k_ref, v_ref, bias_ref, o_ref, *, kv_heads, group, t_len):
    w = WINDOW
    for kv in range(kv_heads):
        q = jnp.concatenate(
            [q_ref[0, :, (kv * group + g) * LANES:(kv * group + g + 1) * LANES]
             for g in range(group)], axis=0).astype(BF16)
        kk = k_ref[0, :, kv * LANES:(kv + 1) * LANES]
        vv = v_ref[0, :, kv * LANES:(kv + 1) * LANES]
        s = lax.dot_general(q, kk, _NT, preferred_element_type=F32)
        s = s + bias_ref[kv * group:(kv + 1) * group].reshape(group * t_len, 2 * w)
        row = lax.broadcasted_iota(jnp.int32, s.shape, 0) & (t_len - 1)
        col = lax.broadcasted_iota(jnp.int32, s.shape, 1)
        dist = w + row - col
        s = jnp.where((dist >= 0) & (dist < w), s, NEG)
        sink_col = jnp.concatenate(
            [jnp.full((t_len, 1), sink_ref[kv * group + g], F32) for g in range(group)], axis=0)
        o = jnp.dot(_sink_softmax(s, sink_col), vv, preferred_element_type=F32)
        for g in range(group):
            hh = kv * group + g
            o_ref[0, :, hh * LANES:(hh + 1) * LANES] = o[g * t_len:(g + 1) * t_len]


def _swa_sample(q, kk, vv, sinks, bias, kv_heads, group):
    bsz, t_len, d = q.shape
    n_keys = kk.shape[1]
    return pl.pallas_call(
        functools.partial(_swa_sample_kernel, kv_heads=kv_heads, group=group, t_len=t_len),
        grid=(bsz,),
        in_specs=[pl.BlockSpec(memory_space=pltpu.SMEM),
                  pl.BlockSpec((1, t_len, d), lambda b: (b, 0, 0)),
                  pl.BlockSpec((1, n_keys, kv_heads * LANES), lambda b: (b, 0, 0)),
                  pl.BlockSpec((1, n_keys, kv_heads * LANES), lambda b: (b, 0, 0)),
                  pl.BlockSpec(bias.shape, lambda b: (0, 0, 0))],
        out_specs=pl.BlockSpec((1, t_len, d), lambda b: (b, 0, 0)),
        out_shape=jax.ShapeDtypeStruct(q.shape, F32),
        compiler_params=_params(("parallel",)),
        name="swa_sample",
    )(sinks, q, kk, vv, bias)


def _sample_bias_tiles(rel_t, n_heads, t_len):
    cols = n_heads * 2 * t_len

    def tile(per_key):
        n_keys = per_key.shape[-1]
        t = jnp.transpose(per_key, (2, 0, 1))[:, :, None, :]
        return jnp.broadcast_to(t, (n_keys, n_heads, 2, t_len)).reshape(n_keys, cols)

    last = tile(rel_t[:, :, :PAGE_SIZE])
    zero = jnp.zeros(((PAGES_PER_STEP - 1) * PAGE_SIZE, cols), F32)
    steps = jnp.stack([jnp.zeros((PAGES_PER_STEP * PAGE_SIZE, cols), F32),
                       jnp.concatenate([zero, last], axis=0)])
    new = tile(rel_t[:, :, PAGE_SIZE:PAGE_SIZE + t_len])
    new = jnp.concatenate([new, jnp.full((PAGE_SIZE - t_len, cols), NEG, F32)], axis=0)
    return steps, new


def _block_diag_queries(q, n_heads, dqk):
    bsz, t_len, _ = q.shape
    qt = jnp.transpose(q.reshape(bsz, t_len, n_heads, 2, dqk), (0, 2, 3, 4, 1))
    comp = jnp.eye(2, dtype=q.dtype)
    blk = jnp.einsum("bhcdt,ce->bhcdet", qt, comp).reshape(bsz, n_heads, 2 * dqk, 2 * t_len)
    head = jnp.eye(n_heads, dtype=q.dtype)
    full = jnp.einsum("bhdk,hg->bhdgk", blk, head)
    return full.reshape(bsz, n_heads * 2 * dqk, n_heads * 2 * t_len)


def _trunk(x, sample, caches, bias, weights, dims):
    (cache_a_k, cache_a_v, cache_b_k, cache_b_v, page_table) = caches
    (norm_ffn1, norm_attn, norm_ffn2, w_ffn_in, w_ffn_out, w_a_qkv, a_q_norm, a_k_norm,
     a_lambda, a_subln, w_a_o, norm_kv, w_b_kv, b_k_norm, w_b_q, b_q_norm, b_sinks, w_b_o) = weights
    depth, n_a, n_heads, kv_heads = dims
    bsz, t_len, d = x.shape
    m = bsz * t_len
    group = n_heads // kv_heads
    dqk = d // n_heads // 2
    hd = d // n_heads
    kv_w = kv_heads * hd
    tm = min(512, m)
    x = x.reshape(m, d)
    a_ks, a_vs = [], []
    kv_k = kv_v = kb16 = vb16 = None
    for li in range(depth):
        if li == n_a:
            kv_k, kb16 = _proj(x, w_b_kv, n=kv_w, g=norm_kv, head_gain=b_k_norm, seg=hd,
                               out_dtypes=(F32, BF16), tm=tm)
            kv_v, vb16 = _proj(x, w_b_kv, n=kv_w, col_off=kv_w, g=norm_kv,
                               out_dtypes=(F32, BF16), tm=tm)
        x = _ffn(x, norm_ffn1[li], w_ffn_in, w_ffn_out, li, 0, tm=tm, tf=512)
        q_dt = F32 if sample else BF16
        if li < n_a:
            lam_init = 0.8 - 0.6 * math.exp(-0.3 * li)
            lam = _diff_lambda(a_lambda[li], lam_init)
            (q,) = _proj(x, w_a_qkv[li], n=d, g=norm_attn[li], head_gain=a_q_norm[li], seg=dqk,
                         scale=dqk ** -0.5 * LOG2E, out_dtypes=(q_dt,), tm=tm)
            k, k16 = _proj(x, w_a_qkv[li], n=d, col_off=d, g=norm_attn[li],
                           head_gain=a_k_norm[li], seg=dqk, out_dtypes=(F32, BF16), tm=tm)
            v, v16 = _proj(x, w_a_qkv[li], n=d, col_off=2 * d, g=norm_attn[li],
                           out_dtypes=(F32, BF16), tm=tm)
            a_ks.append(k.reshape(bsz, t_len, n_heads, hd))
            a_vs.append(v.reshape(bsz, t_len, n_heads, hd))
            if sample:
                n_pool = cache_a_k.shape[1]
                qbd = _block_diag_queries(q.reshape(bsz, t_len, d), n_heads, dqk).astype(BF16)
                pad = ((0, 0), (0, PAGE_SIZE - t_len), (0, 0))
                o = _attn_a_sample(
                    qbd,
                    cache_a_k.reshape((-1,) + cache_a_k.shape[2:]),
                    cache_a_v.reshape((-1,) + cache_a_v.shape[2:]),
                    jnp.pad(k.reshape(bsz, t_len, d), pad), jnp.pad(v.reshape(bsz, t_len, d), pad),
                    page_table, li * n_pool, lam, bias["a_steps"], bias["a_new"],
                    a_subln[li], 1.0 - lam_init, n_heads, t_len)
            else:
                vt = jnp.transpose(v16.reshape(bsz, t_len, d), (0, 2, 1))
                o = _attn_a_prompt(q.reshape(bsz, t_len, d), k16.reshape(bsz, t_len, d), vt, lam,
                                   bias["a_tail"], a_subln[li], 1.0 - lam_init, n_heads)
            (x,) = _proj(o.reshape(m, d), w_a_o[li], n=d, res=x, tm=tm)
        else:
            bi = li - n_a
            (q,) = _proj(x, w_b_q[bi], n=d, g=norm_attn[li], head_gain=b_q_norm[bi], seg=hd,
                         scale=hd ** -0.5, out_dtypes=(q_dt,), tm=tm)
            if sample:
                zpad = jnp.zeros((bsz, WINDOW - t_len, kv_w), BF16)
                kk = jnp.concatenate([cache_b_k.reshape(bsz, WINDOW, kv_w).astype(BF16),
                                      kb16.reshape(bsz, t_len, kv_w), zpad], axis=1)
                vv = jnp.concatenate([cache_b_v.reshape(bsz, WINDOW, kv_w).astype(BF16),
                                      vb16.reshape(bsz, t_len, kv_w), zpad], axis=1)
                o = _swa_sample(q.reshape(bsz, t_len, d), kk, vv, b_sinks[bi], bias["swa_s"],
                                kv_heads, group)
            else:
                o = _swa_prompt(q.reshape(bsz, t_len, d), kb16.reshape(bsz, t_len, kv_w),
                                vb16.reshape(bsz, t_len, kv_w), b_sinks[bi], bias["swa_p"],
                                kv_heads, group)
            (x,) = _proj(o.reshape(m, d), w_b_o[bi], n=d, res=x, tm=tm)
        x = _ffn(x, norm_ffn2[li], w_ffn_in, w_ffn_out, li, 1, tm=tm, tf=512)
    kv_k = kv_k.reshape(bsz, t_len, kv_heads, hd)
    kv_v = kv_v.reshape(bsz, t_len, kv_heads, hd)
    if sample:
        kv_k = jnp.concatenate([cache_b_k, kv_k], axis=1)
        kv_v = jnp.concatenate([cache_b_v, kv_v], axis=1)
    keep = min(WINDOW, kv_k.shape[1])
    return (x.reshape(bsz, t_len, d), jnp.stack(a_ks), jnp.stack(a_vs),
            kv_k[:, -keep:], kv_v[:, -keep:])


def kernel(x_prompt, x_sample, cache_a_k, cache_a_v, cache_b_k, cache_b_v, page_table, rel_bias_table, norm_ffn1, norm_attn, norm_ffn2, w_ffn_in, w_ffn_out, w_a_qkv, a_q_norm, a_k_norm, a_lambda, a_subln, w_a_o, norm_kv, w_b_kv, b_k_norm, w_b_q, b_q_norm, b_sinks, w_b_o):
    depth = norm_ffn1.shape[0]
    n_a = w_a_qkv.shape[0]
    n_heads = rel_bias_table.shape[1]
    kv_heads = cache_b_k.shape[2]
    t_len = x_sample.shape[1]
    dims = (depth, n_a, n_heads, kv_heads)

    tq = TQ_A
    kc = np.arange(2 * tq)[:, None]
    qr = np.arange(tq)[None, :]
    tail0 = _bias_tiles(rel_bias_table, _bucket_np(qr - kc), rel_far=True, scale=LOG2E)
    tail1 = _bias_tiles(rel_bias_table, _bucket_np(tq + qr - kc), rel_far=True, scale=LOG2E)
    rw = np.arange(WINDOW)[:, None]
    cw = np.arange(2 * WINDOW)[None, :]
    swa_idx = _bucket_np(WINDOW + rw - cw)
    swa = _bias_tiles(rel_bias_table, np.maximum(swa_idx, 0))
    rel_t = _bias_tiles(rel_bias_table, swa_idx[:t_len], rel_far=True, scale=LOG2E)
    a_steps, a_new = _sample_bias_tiles(rel_t, n_heads, t_len)
    bias = {"a_tail": jnp.stack([tail0, tail1], axis=1), "swa_p": swa, "swa_s": swa[:, :t_len],
            "a_steps": a_steps, "a_new": a_new}

    wcast = lambda w: w.astype(BF16)
    weights = (norm_ffn1, norm_attn, norm_ffn2, wcast(w_ffn_in), wcast(w_ffn_out), wcast(w_a_qkv),
               a_q_norm, a_k_norm, a_lambda, a_subln, wcast(w_a_o), norm_kv, wcast(w_b_kv),
               b_k_norm, wcast(w_b_q), b_q_norm, b_sinks, wcast(w_b_o))
    none5 = (None,) * 5
    y_p, ak_p, av_p, bk_p, bv_p = _trunk(x_prompt, False, none5, bias, weights, dims)
    y_s, ak_s, av_s, bk_s, bv_s = _trunk(
        x_sample, True, (cache_a_k, cache_a_v, cache_b_k, cache_b_v, page_table), bias, weights, dims)
    return (y_p, y_s, ak_p, av_p, ak_s, av_s, bk_p, bv_p, bk_s, bv_s)
```

```python
import functools
import math

import numpy as np
import jax
import jax.numpy as jnp
from jax import lax
from jax.experimental import pallas as pl
from jax.experimental.pallas import tpu as pltpu

F32 = jnp.float32
BF16 = jnp.bfloat16

EPS = 1e-6
NEG = -1e30
LOG2E = math.log2(math.e)
N_BUCKETS = 32
MAX_DISTANCE = 128
PAGE_SIZE = 128
WINDOW = 128
LANES = 128
VMEM_LIMIT = 56 * 1024 * 1024

TQ_A = 256
HEADS_A = 4
PAGES_PER_STEP = 4
RING_SLOTS = 3

_NT = (((1,), (1,)), ((), ()))


def _params(sem):
    return pltpu.CompilerParams(dimension_semantics=sem, vmem_limit_bytes=VMEM_LIMIT)


def _bucket_np(dist):
    n = np.maximum(dist, 0)
    max_exact = N_BUCKETS // 2
    nf = np.maximum(n, 1).astype(np.float32)
    large = max_exact + (np.log(nf / np.float32(max_exact))
                         / np.float32(math.log(MAX_DISTANCE / max_exact))
                         * np.float32(N_BUCKETS - max_exact)).astype(np.int32)
    bucket = np.where(n < max_exact, n, np.minimum(large, N_BUCKETS - 1))
    return np.where(dist < 0, -1, bucket).astype(np.int32)


def _bias_kernel(tbl_ref, idx_ref, o_ref, *, rel_far, scale):
    h = pl.program_id(0)
    idx = idx_ref[...]
    acc = jnp.zeros(idx.shape, F32)
    for b in range(N_BUCKETS):
        acc = jnp.where(idx == b, tbl_ref[b, h], acc)
    if rel_far:
        acc = jnp.where(idx < 0, NEG, (acc - tbl_ref[N_BUCKETS - 1, h]) * scale)
    o_ref[0] = acc


def _bias_tiles(table, idx_np, *, rel_far=False, scale=1.0):
    n_heads = table.shape[1]
    r, c = idx_np.shape
    return pl.pallas_call(
        functools.partial(_bias_kernel, rel_far=rel_far, scale=scale),
        grid=(n_heads,),
        in_specs=[pl.BlockSpec(memory_space=pltpu.SMEM),
                  pl.BlockSpec((r, c), lambda h: (0, 0))],
        out_specs=pl.BlockSpec((1, r, c), lambda h: (h, 0, 0)),
        out_shape=jax.ShapeDtypeStruct((n_heads, r, c), F32),
        compiler_params=_params(("arbitrary",)),
        name="bias_tiles",
    )(table, jnp.asarray(idx_np))


def _lam_kernel(al_ref, o_ref, *, lam_init):
    a = al_ref[...]
    s1 = jnp.sum(a[0:1] * a[1:2], axis=-1, keepdims=True)
    s2 = jnp.sum(a[2:3] * a[3:4], axis=-1, keepdims=True)
    lam = jnp.exp(s1) - jnp.exp(s2) + lam_init
    o_ref[...] = jnp.broadcast_to(lam, o_ref.shape)


def _diff_lambda(lam_params, lam_init):
    out = pl.pallas_call(
        functools.partial(_lam_kernel, lam_init=lam_init),
        out_shape=jax.ShapeDtypeStruct((8, LANES), F32),
        name="diff_lambda",
    )(lam_params)
    return out[0, :1]


def _ffn_kernel(x_ref, g_ref, wa_ref, wb_ref, wo_ref, o_ref, xn_ref, acc_ref):
    f = pl.program_id(1)

    @pl.when(f == 0)
    def _():
        xf = x_ref[...]
        y = xf * lax.rsqrt(jnp.mean(xf * xf, axis=-1, keepdims=True) + EPS) * g_ref[...]
        xn_ref[...] = y.astype(BF16)
        acc_ref[...] = jnp.zeros_like(acc_ref)

    xn = xn_ref[...]
    a = jnp.dot(xn, wa_ref[...], preferred_element_type=F32)
    b = jnp.dot(xn, wb_ref[...], preferred_element_type=F32)
    hid = (a * (1.0 / (1.0 + jnp.exp(-a))) * b).astype(BF16)
    acc_ref[...] += jnp.dot(hid, wo_ref[...], preferred_element_type=F32)

    @pl.when(f == pl.num_programs(1) - 1)
    def _():
        o_ref[...] = x_ref[...] + 0.5 * acc_ref[...]


def _ffn(x, g, w_in, w_out, li, k, *, tm, tf):
    m, d = x.shape
    d_ff = w_out.shape[2]
    nf = d_ff // tf
    return pl.pallas_call(
        _ffn_kernel,
        grid=(m // tm, nf),
        in_specs=[pl.BlockSpec((tm, d), lambda i, f: (i, 0)),
                  pl.BlockSpec((1, d), lambda i, f: (0, 0)),
                  pl.BlockSpec((None, None, d, tf), lambda i, f: (li, k, 0, f)),
                  pl.BlockSpec((None, None, d, tf), lambda i, f: (li, k, 0, f + nf)),
                  pl.BlockSpec((None, None, tf, d), lambda i, f: (li, k, f, 0))],
        out_specs=pl.BlockSpec((tm, d), lambda i, f: (i, 0)),
        out_shape=jax.ShapeDtypeStruct((m, d), F32),
        scratch_shapes=[pltpu.VMEM((tm, d), BF16), pltpu.VMEM((tm, d), F32)],
        compiler_params=_params(("parallel", "arbitrary")),
        name="ffn",
    )(x, g.reshape(1, d), w_in, w_in, w_out)


def _proj_kernel(*refs, rms, seg, scale, residual, n_out):
    refs = list(refs)
    x_ref = refs.pop(0)
    g_ref = refs.pop(0) if rms else None
    w_ref = refs.pop(0)
    res_ref = refs.pop(0) if residual else None
    hg_ref = refs.pop(0) if seg else None
    o_refs = [refs.pop(0) for _ in range(n_out)]
    xn_ref = refs.pop(0)

    @pl.when(pl.program_id(1) == 0)
    def _():
        xf = x_ref[...].astype(F32)
        if rms:
            xf = xf * lax.rsqrt(jnp.mean(xf * xf, axis=-1, keepdims=True) + EPS) * g_ref[...]
        xn_ref[...] = xf.astype(BF16)

    acc = jnp.dot(xn_ref[...], w_ref[...], preferred_element_type=F32)
    tm, tn = acc.shape
    if residual:
        acc = res_ref[...] + acc
    if not seg:
        for o_ref in o_refs:
            o_ref[...] = acc.astype(o_ref.dtype)
        return
    lane = lax.broadcasted_iota(jnp.int32, (tm, LANES), 1)
    low = lane < seg
    gain = hg_ref[...] * scale
    for cb in range(tn // LANES):
        blk = acc[:, cb * LANES:(cb + 1) * LANES]
        sq = blk * blk
        if seg == LANES:
            ms = jnp.mean(sq, axis=-1, keepdims=True)
        else:
            s_lo = jnp.sum(jnp.where(low, sq, 0.0), axis=-1, keepdims=True)
            s_hi = jnp.sum(jnp.where(low, 0.0, sq), axis=-1, keepdims=True)
            ms = jnp.where(low, s_lo, s_hi) * (1.0 / seg)
        y = blk * lax.rsqrt(ms + EPS) * gain
        for o_ref in o_refs:
            o_ref[:, cb * LANES:(cb + 1) * LANES] = y.astype(o_ref.dtype)


def _proj(x, w, *, n, col_off=0, g=None, head_gain=None, seg=0, scale=1.0, res=None,
          out_dtypes=(F32,), tm=512, tn=2048):
    m, d = x.shape
    tm = min(tm, m)
    tn = min(tn, n)
    off = col_off // tn
    ins = [x]
    specs = [pl.BlockSpec((tm, d), lambda i, j: (i, 0))]
    if g is not None:
        ins.append(g.reshape(1, d))
        specs.append(pl.BlockSpec((1, d), lambda i, j: (0, 0)))
    ins.append(w)
    specs.append(pl.BlockSpec((d, tn), lambda i, j: (0, j + off)))
    if res is not None:
        ins.append(res)
        specs.append(pl.BlockSpec((tm, tn), lambda i, j: (i, j)))
    if seg:
        hg = jnp.tile(head_gain.astype(F32), LANES // seg).reshape(1, LANES)
        ins.append(hg)
        specs.append(pl.BlockSpec((1, LANES), lambda i, j: (0, 0)))
    outs = pl.pallas_call(
        functools.partial(_proj_kernel, rms=g is not None, seg=seg, scale=scale,
                          residual=res is not None, n_out=len(out_dtypes)),
        grid=(m // tm, n // tn),
        in_specs=specs,
        out_specs=[pl.BlockSpec((tm, tn), lambda i, j: (i, j)) for _ in out_dtypes],
        out_shape=[jax.ShapeDtypeStruct((m, n), dt) for dt in out_dtypes],
        scratch_shapes=[pltpu.VMEM((tm, d), BF16)],
        compiler_params=_params(("parallel", "arbitrary")),
        name="proj",
    )(*ins)
    return outs


def _attn_a_prompt_kernel(lam_ref, q_ref, k_ref, vt_ref, bt_ref, sg_ref, o_ref,
                          m_ref, l_ref, acc_ref, *, tq, dqk, n_hd, post_scale):
    i = pl.program_id(2)
    lane = lax.broadcasted_iota(jnp.int32, (tq, LANES), 1)
    qqs = []
    for g in range(n_hd):
        q = q_ref[0, :, g * LANES:(g + 1) * LANES]
        zero = jnp.zeros_like(q)
        qqs.append(jnp.concatenate([jnp.where(lane < dqk, q, zero),
                                    jnp.where(lane < dqk, zero, q)], axis=0))
    m_ref[...] = jnp.full(m_ref.shape, NEG, F32)
    l_ref[...] = jnp.zeros(l_ref.shape, F32)
    acc_ref[...] = jnp.zeros(acc_ref.shape, F32)

    def update(start, size, biases):
        for g in range(n_hd):
            kblk = k_ref[0, pl.ds(start, size), g * LANES:(g + 1) * LANES]
            vtblk = vt_ref[0, g * LANES:(g + 1) * LANES, pl.ds(start, size)]
            s = lax.dot_general(kblk, qqs[g], _NT, preferred_element_type=F32)
            if biases is not None:
                s = s + biases[g]
            m_prev = m_ref[g]
            m_new = jnp.maximum(m_prev, jnp.max(s, axis=0, keepdims=True))
            p = jnp.exp2(s - m_new)
            alpha = jnp.exp2(m_prev - m_new)
            m_ref[g] = m_new
            l_ref[g] = alpha * l_ref[g] + jnp.sum(p, axis=0, keepdims=True)
            acc_ref[g] = alpha * acc_ref[g] + jnp.dot(vtblk, p.astype(BF16),
                                                      preferred_element_type=F32)

    tails = []
    for g in range(n_hd):
        b = bt_ref[g, 0]
        tails.append(jnp.concatenate([b, b], axis=1))
    update(pl.multiple_of(jnp.maximum(i - 1, 0) * tq, tq), 2 * tq, tails)

    def far(f, c):
        update(pl.multiple_of(f * 2 * tq, 2 * tq), 2 * tq, None)
        return c
    lax.fori_loop(0, jnp.maximum(i - 1, 0) // 2, far, 0)

    @pl.when(jnp.logical_and(i >= 2, (i & 1) == 0))
    def _():
        update(pl.multiple_of((i - 2) * tq, tq), tq, None)

    gain = sg_ref[...] * post_scale
    for g in range(n_hd):
        o = acc_ref[g] / l_ref[g]
        d = o[:, :tq] - lam_ref[0] * o[:, tq:]
        y = d * lax.rsqrt(jnp.mean(d * d, axis=0, keepdims=True) + EPS) * gain
        o_ref[0, :, g * LANES:(g + 1) * LANES] = y.T.astype(o_ref.dtype)


def _attn_a_prompt(q, k, vt, lam, bias_tail, subln, post_scale, n_heads):
    bsz, s_len, _ = q.shape
    tq, n_hd = TQ_A, HEADS_A
    gw = n_hd * LANES
    return pl.pallas_call(
        functools.partial(_attn_a_prompt_kernel, tq=tq, dqk=LANES // 2, n_hd=n_hd,
                          post_scale=post_scale),
        grid=(bsz, n_heads // n_hd, s_len // tq),
        in_specs=[pl.BlockSpec(memory_space=pltpu.SMEM),
                  pl.BlockSpec((1, tq, gw), lambda b, h, i: (b, i, h)),
                  pl.BlockSpec((1, s_len, gw), lambda b, h, i: (b, 0, h)),
                  pl.BlockSpec((1, gw, s_len), lambda b, h, i: (b, h, 0)),
                  pl.BlockSpec((n_hd, 1, 2 * tq, tq), lambda b, h, i: (h, jnp.minimum(i, 1), 0, 0)),
                  pl.BlockSpec((LANES, 1), lambda b, h, i: (0, 0))],
        out_specs=pl.BlockSpec((1, tq, gw), lambda b, h, i: (b, i, h)),
        out_shape=jax.ShapeDtypeStruct(q.shape, BF16),
        scratch_shapes=[pltpu.VMEM((n_hd, 1, 2 * tq), F32), pltpu.VMEM((n_hd, 1, 2 * tq), F32),
                        pltpu.VMEM((n_hd, LANES, 2 * tq), F32)],
        compiler_params=_params(("parallel", "parallel", "arbitrary")),
        name="attn_a_prompt",
    )(lam, q, k, vt, bias_tail, subln.reshape(LANES, 1))


def _attn_a_sample_kernel(pt_ref, lam_ref, qbd_ref, k_hbm, v_hbm, kn_ref, vn_ref, bias_ref,
                          bnew_ref, sg_ref, o_ref, kbuf, vbuf, sem, m_ref, l_ref, acc_ref,
                          *, n_pg, n_heads, t_len, page_off, post_scale):
    b = pl.program_id(0)
    j = pl.program_id(1)
    n_steps = pl.num_programs(1)
    n_total = pl.num_programs(0) * n_steps
    g = b * n_steps + j
    slot = g % RING_SLOTS
    rows = 2 * t_len
    cols = n_heads * rows

    def page_copies(step):
        bb = step // n_steps
        jj = step % n_steps
        sl = step % RING_SLOTS
        out = []
        for p in range(n_pg):
            page = pt_ref[bb, jj * n_pg + p] + page_off
            for h in range(n_heads):
                out.append(pltpu.make_async_copy(k_hbm.at[page, :, h, :], kbuf.at[sl, p, h],
                                                 sem.at[sl]))
                out.append(pltpu.make_async_copy(v_hbm.at[page, :, h, :], vbuf.at[sl, p, h],
                                                 sem.at[sl]))
        return out

    @pl.when(g == 0)
    def _():
        for ahead in range(RING_SLOTS - 1):
            for c in page_copies(g + ahead):
                c.start()

    @pl.when(g + RING_SLOTS - 1 < n_total)
    def _():
        for c in page_copies(g + RING_SLOTS - 1):
            c.start()

    @pl.when(j == 0)
    def _():
        m_ref[...] = jnp.full(m_ref.shape, NEG, F32)
        l_ref[...] = jnp.zeros(l_ref.shape, F32)
        acc_ref[...] = jnp.zeros(acc_ref.shape, F32)

    for c in page_copies(g):
        c.wait()

    def process(k_tiles, v_tiles, bias):
        n_p = len(k_tiles)
        kk = jnp.concatenate(
            [jnp.concatenate([t().astype(BF16) for t in k_tiles[p]], axis=1) for p in range(n_p)],
            axis=0)
        s = jnp.dot(kk, qbd_ref[0], preferred_element_type=F32) + bias
        m_prev = m_ref[...]
        m_new = jnp.maximum(m_prev, jnp.max(s, axis=0, keepdims=True))
        p_t = jnp.exp2(s - m_new)
        alpha = jnp.exp2(m_prev - m_new)
        m_ref[...] = m_new
        l_ref[...] = alpha * l_ref[...] + jnp.sum(p_t, axis=0, keepdims=True)
        pb = p_t.T.astype(BF16)
        heads = []
        for h in range(n_heads):
            vh = jnp.concatenate([v_tiles[p][h]().astype(BF16) for p in range(n_p)], axis=0)
            heads.append(jnp.dot(pb[h * rows:(h + 1) * rows], vh, preferred_element_type=F32))
        alpha_col = jnp.broadcast_to(alpha, (8, cols)).T[:, :1]
        acc_ref[...] = alpha_col * acc_ref[...] + jnp.concatenate(heads, axis=0)

    k_tiles = [[(lambda p=p, h=h: kbuf[slot, p, h]) for h in range(n_heads)] for p in range(n_pg)]
    v_tiles = [[(lambda p=p, h=h: vbuf[slot, p, h]) for h in range(n_heads)] for p in range(n_pg)]
    process(k_tiles, v_tiles, bias_ref[0])

    @pl.when(j == n_steps - 1)
    def _():
        kn = [[(lambda h=h: kn_ref[0, :, h * LANES:(h + 1) * LANES]) for h in range(n_heads)]]
        vn = [[(lambda h=h: vn_ref[0, :, h * LANES:(h + 1) * LANES]) for h in range(n_heads)]]
        process(kn, vn, bnew_ref[...])
        l_col = jnp.broadcast_to(l_ref[...], (8, cols)).T[:, :1]
        o = acc_ref[...] / l_col
        gain = sg_ref[...] * post_scale
        for h in range(n_heads):
            d = o[h * rows:h * rows + t_len] - lam_ref[0] * o[h * rows + t_len:(h + 1) * rows]
            y = d * lax.rsqrt(jnp.mean(d * d, axis=-1, keepdims=True) + EPS) * gain
            o_ref[0, :, h * LANES:(h + 1) * LANES] = y


def _attn_a_sample(qbd, cache_k, cache_v, k_new, v_new, page_table, page_off, lam,
                   bias_steps, bias_new, subln, post_scale, n_heads, t_len):
    bsz, n_pages = page_table.shape
    n_pg = PAGES_PER_STEP
    n_steps = n_pages // n_pg
    d, cols = qbd.shape[1:]
    hd = d // n_heads
    grid_spec = pltpu.PrefetchScalarGridSpec(
        num_scalar_prefetch=1,
        grid=(bsz, n_steps),
        in_specs=[pl.BlockSpec(memory_space=pltpu.SMEM),
                  pl.BlockSpec((1, d, cols), lambda b, j, pt: (b, 0, 0)),
                  pl.BlockSpec(memory_space=pl.ANY),
                  pl.BlockSpec(memory_space=pl.ANY),
                  pl.BlockSpec((1, PAGE_SIZE, d), lambda b, j, pt: (b, 0, 0)),
                  pl.BlockSpec((1, PAGE_SIZE, d), lambda b, j, pt: (b, 0, 0)),
                  pl.BlockSpec((1, n_pg * PAGE_SIZE, cols),
                               lambda b, j, pt: (jnp.where(j == n_steps - 1, 1, 0), 0, 0)),
                  pl.BlockSpec((PAGE_SIZE, cols), lambda b, j, pt: (0, 0)),
                  pl.BlockSpec((1, LANES), lambda b, j, pt: (0, 0))],
        out_specs=pl.BlockSpec((1, t_len, d), lambda b, j, pt: (b, 0, 0)),
        scratch_shapes=[pltpu.VMEM((RING_SLOTS, n_pg, n_heads, PAGE_SIZE, hd), F32),
                        pltpu.VMEM((RING_SLOTS, n_pg, n_heads, PAGE_SIZE, hd), F32),
                        pltpu.SemaphoreType.DMA((RING_SLOTS,)),
                        pltpu.VMEM((1, cols), F32), pltpu.VMEM((1, cols), F32),
                        pltpu.VMEM((cols, hd), F32)],
    )
    return pl.pallas_call(
        functools.partial(_attn_a_sample_kernel, n_pg=n_pg, n_heads=n_heads, t_len=t_len,
                          page_off=page_off, post_scale=post_scale),
        grid_spec=grid_spec,
        out_shape=jax.ShapeDtypeStruct((bsz, t_len, d), F32),
        compiler_params=_params(("arbitrary", "arbitrary")),
        name="attn_a_sample",
    )(page_table, lam, qbd, cache_k, cache_v, k_new, v_new, bias_steps, bias_new,
      subln.reshape(1, LANES))


def _sink_softmax(s, sink_col):
    m = jnp.maximum(jnp.max(s, axis=-1, keepdims=True), sink_col)
    e = jnp.exp(s - m)
    denom = jnp.sum(e, axis=-1, keepdims=True) + jnp.exp(sink_col - m)
    return (e / denom).astype(BF16)


def _swa_prompt_kernel(sink_ref, q_ref, kp_ref, kc_ref, vtp_ref, vtc_ref, bias_ref, o_ref,
                       *, kv_heads, group):
    n = pl.program_id(1)
    w = WINDOW
    key = lax.broadcasted_iota(jnp.int32, (2 * w, group * w), 0)
    qry = lax.broadcasted_iota(jnp.int32, (2 * w, group * w), 1) & (w - 1)
    dist = w + qry - key
    first_key = jnp.where(n > 0, 0, w)
    valid = (dist >= 0) & (dist < w) & (key >= first_key)
    for kv in range(kv_heads):
        hs = [kv * group + g for g in range(group)]
        q = jnp.concatenate([q_ref[0, :, h * LANES:(h + 1) * LANES] for h in hs], axis=0)
        ksl = slice(kv * LANES, (kv + 1) * LANES)
        kcat = jnp.concatenate([kp_ref[0, :, ksl], kc_ref[0, :, ksl]], axis=0)
        vtcat = jnp.concatenate([vtp_ref[0, ksl, :], vtc_ref[0, ksl, :]], axis=1)
        s = lax.dot_general(kcat, q, _NT, preferred_element_type=F32)
        s = s + jnp.concatenate([bias_ref[h] for h in hs], axis=1)
        s = jnp.where(valid, s, NEG)
        sink = jnp.concatenate([jnp.full((1, w), sink_ref[h], F32) for h in hs], axis=1)
        m = jnp.maximum(jnp.max(s, axis=0, keepdims=True), sink)
        e = jnp.exp(s - m)
        denom = jnp.sum(e, axis=0, keepdims=True) + jnp.exp(sink - m)
        o_t = jnp.dot(vtcat, (e / denom).astype(BF16), preferred_element_type=F32)
        for g, h in enumerate(hs):
            o_ref[0, :, h * LANES:(h + 1) * LANES] = o_t[:, g * w:(g + 1) * w].T.astype(o_ref.dtype)


def _swa_prompt(q, k, vt, sinks, bias, kv_heads, group):
    bsz, s_len, d = q.shape
    nb = s_len // WINDOW
    kw = kv_heads * LANES
    prev = lambda b, n: (b, jnp.maximum(n - 1, 0), 0)
    cur = lambda b, n: (b, n, 0)
    prev_t = lambda b, n: (b, 0, jnp.maximum(n - 1, 0))
    cur_t = lambda b, n: (b, 0, n)
    return pl.pallas_call(
        functools.partial(_swa_prompt_kernel, kv_heads=kv_heads, group=group),
        grid=(bsz, nb),
        in_specs=[pl.BlockSpec(memory_space=pltpu.SMEM),
                  pl.BlockSpec((1, WINDOW, d), cur),
                  pl.BlockSpec((1, WINDOW, kw), prev),
                  pl.BlockSpec((1, WINDOW, kw), cur),
                  pl.BlockSpec((1, kw, WINDOW), prev_t),
                  pl.BlockSpec((1, kw, WINDOW), cur_t),
                  pl.BlockSpec(bias.shape, lambda b, n: (0, 0, 0))],
        out_specs=pl.BlockSpec((1, WINDOW, d), cur),
        out_shape=jax.ShapeDtypeStruct(q.shape, BF16),
        compiler_params=_params(("parallel", "arbitrary")),
        name="swa_prompt",
    )(sinks, q, k, k, vt, vt, bias)


def _swa_sample_kernel(sink_ref, q_ref, k_ref, v_ref, bias_ref, o_ref, *, kv_heads, group, t_len):
    w = WINDOW
    for kv in range(kv_heads):
        q = jnp.concatenate(
            [q_ref[0, :, (kv * group + g) * LANES:(kv * group + g + 1) * LANES]
             for g in range(group)], axis=0).astype(BF16)
        kk = k_ref[0, :, kv * LANES:(kv + 1) * LANES]
        vv = v_ref[0, :, kv * LANES:(kv + 1) * LANES]
        s = lax.dot_general(q, kk, _NT, preferred_element_type=F32)
        s = s + bias_ref[kv * group:(kv + 1) * group].reshape(group * t_len, 2 * w)
        row = lax.broadcasted_iota(jnp.int32, s.shape, 0) & (t_len - 1)
        col = lax.broadcasted_iota(jnp.int32, s.shape, 1)
        dist = w + row - col
        s = jnp.where((dist >= 0) & (dist < w), s, NEG)
        sink_col = jnp.concatenate(
            [jnp.full((t_len, 1), sink_ref[kv * group + g], F32) for g in range(group)], axis=0)
        o = jnp.dot(_sink_softmax(s, sink_col), vv, preferred_element_type=F32)
        for g in range(group):
            hh = kv * group + g
            o_ref[0, :, hh * LANES:(hh + 1) * LANES] = o[g * t_len:(g + 1) * t_len]


def _swa_sample(q, kk, vv, sinks, bias, kv_heads, group):
    bsz, t_len, d = q.shape
    n_keys = kk.shape[1]
    return pl.pallas_call(
        functools.partial(_swa_sample_kernel, kv_heads=kv_heads, group=group, t_len=t_len),
        grid=(bsz,),
        in_specs=[pl.BlockSpec(memory_space=pltpu.SMEM),
                  pl.BlockSpec((1, t_len, d), lambda b: (b, 0, 0)),
                  pl.BlockSpec((1, n_keys, kv_heads * LANES), lambda b: (b, 0, 0)),
                  pl.BlockSpec((1, n_keys, kv_heads * LANES), lambda b: (b, 0, 0)),
                  pl.BlockSpec(bias.shape, lambda b: (0, 0, 0))],
        out_specs=pl.BlockSpec((1, t_len, d), lambda b: (b, 0, 0)),
        out_shape=jax.ShapeDtypeStruct(q.shape, F32),
        compiler_params=_params(("parallel",)),
        name="swa_sample",
    )(sinks, q, kk, vv, bias)


def _sample_bias_tiles(rel_t, n_heads, t_len):
    cols = n_heads * 2 * t_len

    def tile(per_key):
        n_keys = per_key.shape[-1]
        t = jnp.transpose(per_key, (2, 0, 1))[:, :, None, :]
        return jnp.broadcast_to(t, (n_keys, n_heads, 2, t_len)).reshape(n_keys, cols)

    last = tile(rel_t[:, :, :PAGE_SIZE])
    zero = jnp.zeros(((PAGES_PER_STEP - 1) * PAGE_SIZE, cols), F32)
    steps = jnp.stack([jnp.zeros((PAGES_PER_STEP * PAGE_SIZE, cols), F32),
                       jnp.concatenate([zero, last], axis=0)])
    new = tile(rel_t[:, :, PAGE_SIZE:PAGE_SIZE + t_len])
    new = jnp.concatenate([new, jnp.full((PAGE_SIZE - t_len, cols), NEG, F32)], axis=0)
    return steps, new


def _block_diag_queries(q, n_heads, dqk):
    bsz, t_len, _ = q.shape
    qt = jnp.transpose(q.reshape(bsz, t_len, n_heads, 2, dqk), (0, 2, 3, 4, 1))
    comp = jnp.eye(2, dtype=q.dtype)
    blk = jnp.einsum("bhcdt,ce->bhcdet", qt, comp).reshape(bsz, n_heads, 2 * dqk, 2 * t_len)
    head = jnp.eye(n_heads, dtype=q.dtype)
    full = jnp.einsum("bhdk,hg->bhdgk", blk, head)
    return full.reshape(bsz, n_heads * 2 * dqk, n_heads * 2 * t_len)


def _trunk(x, sample, caches, bias, weights, dims):
    (cache_a_k, cache_a_v, cache_b_k, cache_b_v, page_table) = caches
    (norm_ffn1, norm_attn, norm_ffn2, w_ffn_in, w_ffn_out, w_a_qkv, a_q_norm, a_k_norm,
     a_lambda, a_subln, w_a_o, norm_kv, w_b_kv, b_k_norm, w_b_q, b_q_norm, b_sinks, w_b_o) = weights
    depth, n_a, n_heads, kv_heads = dims
    bsz, t_len, d = x.shape
    m = bsz * t_len
    group = n_heads // kv_heads
    dqk = d // n_heads // 2
    hd = d // n_heads
    kv_w = kv_heads * hd
    tm = min(512, m)
    x = x.reshape(m, d)
    a_ks, a_vs = [], []
    kv_k = kv_v = kb16 = vb16 = None
    for li in range(depth):
        if li == n_a:
            kv_k, kb16 = _proj(x, w_b_kv, n=kv_w, g=norm_kv, head_gain=b_k_norm, seg=hd,
                               out_dtypes=(F32, BF16), tm=tm)
            kv_v, vb16 = _proj(x, w_b_kv, n=kv_w, col_off=kv_w, g=norm_kv,
                               out_dtypes=(F32, BF16), tm=tm)
        x = _ffn(x, norm_ffn1[li], w_ffn_in, w_ffn_out, li, 0, tm=tm, tf=512)
        q_dt = F32 if sample else BF16
        if li < n_a:
            lam_init = 0.8 - 0.6 * math.exp(-0.3 * li)
            lam = _diff_lambda(a_lambda[li], lam_init)
            (q,) = _proj(x, w_a_qkv[li], n=d, g=norm_attn[li], head_gain=a_q_norm[li], seg=dqk,
                         scale=dqk ** -0.5 * LOG2E, out_dtypes=(q_dt,), tm=tm)
            k, k16 = _proj(x, w_a_qkv[li], n=d, col_off=d, g=norm_attn[li],
                           head_gain=a_k_norm[li], seg=dqk, out_dtypes=(F32, BF16), tm=tm)
            v, v16 = _proj(x, w_a_qkv[li], n=d, col_off=2 * d, g=norm_attn[li],
                           out_dtypes=(F32, BF16), tm=tm)
            a_ks.append(k.reshape(bsz, t_len, n_heads, hd))
            a_vs.append(v.reshape(bsz, t_len, n_heads, hd))
            if sample:
                n_pool = cache_a_k.shape[1]
                qbd = _block_diag_queries(q.reshape(bsz, t_len, d), n_heads, dqk).astype(BF16)
                pad = ((0, 0), (0, PAGE_SIZE - t_len), (0, 0))
                o = _attn_a_sample(
                    qbd,
                    cache_a_k.reshape((-1,) + cache_a_k.shape[2:]),
                    cache_a_v.reshape((-1,) + cache_a_v.shape[2:]),
                    jnp.pad(k.reshape(bsz, t_len, d), pad), jnp.pad(v.reshape(bsz, t_len, d), pad),
                    page_table, li * n_pool, lam, bias["a_steps"], bias["a_new"],
                    a_subln[li], 1.0 - lam_init, n_heads, t_len)
            else:
                vt = jnp.transpose(v16.reshape(bsz, t_len, d), (0, 2, 1))
                o = _attn_a_prompt(q.reshape(bsz, t_len, d), k16.reshape(bsz, t_len, d), vt, lam,
                                   bias["a_tail"], a_subln[li], 1.0 - lam_init, n_heads)
            (x,) = _proj(o.reshape(m, d), w_a_o[li], n=d, res=x, tm=tm)
        else:
            bi = li - n_a
            (q,) = _proj(x, w_b_q[bi], n=d, g=norm_attn[li], head_gain=b_q_norm[bi], seg=hd,
                         scale=hd ** -0.5, out_dtypes=(q_dt,), tm=tm)
            if sample:
                zpad = jnp.zeros((bsz, WINDOW - t_len, kv_w), BF16)
                kk = jnp.concatenate([cache_b_k.reshape(bsz, WINDOW, kv_w).astype(BF16),
                                      kb16.reshape(bsz, t_len, kv_w), zpad], axis=1)
                vv = jnp.concatenate([cache_b_v.reshape(bsz, WINDOW, kv_w).astype(BF16),
                                      vb16.reshape(bsz, t_len, kv_w), zpad], axis=1)
                o = _swa_sample(q.reshape(bsz, t_len, d), kk, vv, b_sinks[bi], bias["swa_s"],
                                kv_heads, group)
            else:
                vbt = jnp.transpose(vb16.reshape(bsz, t_len, kv_w), (0, 2, 1))
                o = _swa_prompt(q.reshape(bsz, t_len, d), kb16.reshape(bsz, t_len, kv_w), vbt,
                                b_sinks[bi], bias["swa_p"], kv_heads, group)
            (x,) = _proj(o.reshape(m, d), w_b_o[bi], n=d, res=x, tm=tm)
        x = _ffn(x, norm_ffn2[li], w_ffn_in, w_ffn_out, li, 1, tm=tm, tf=512)
    kv_k = kv_k.reshape(bsz, t_len, kv_heads, hd)
    kv_v = kv_v.reshape(bsz, t_len, kv_heads, hd)
    if sample:
        kv_k = jnp.concatenate([cache_b_k, kv_k], axis=1)
        kv_v = jnp.concatenate([cache_b_v, kv_v], axis=1)
    keep = min(WINDOW, kv_k.shape[1])
    return (x.reshape(bsz, t_len, d), jnp.stack(a_ks), jnp.stack(a_vs),
            kv_k[:, -keep:], kv_v[:, -keep:])


def kernel(x_prompt, x_sample, cache_a_k, cache_a_v, cache_b_k, cache_b_v, page_table, rel_bias_table, norm_ffn1, norm_attn, norm_ffn2, w_ffn_in, w_ffn_out, w_a_qkv, a_q_norm, a_k_norm, a_lambda, a_subln, w_a_o, norm_kv, w_b_kv, b_k_norm, w_b_q, b_q_norm, b_sinks, w_b_o):
    depth = norm_ffn1.shape[0]
    n_a = w_a_qkv.shape[0]
    n_heads = rel_bias_table.shape[1]
    kv_heads = cache_b_k.shape[2]
    t_len = x_sample.shape[1]
    dims = (depth, n_a, n_heads, kv_heads)

    tq = TQ_A
    kc = np.arange(2 * tq)[:, None]
    qr = np.arange(tq)[None, :]
    tail0 = _bias_tiles(rel_bias_table, _bucket_np(qr - kc), rel_far=True, scale=LOG2E)
    tail1 = _bias_tiles(rel_bias_table, _bucket_np(tq + qr - kc), rel_far=True, scale=LOG2E)
    rw = np.arange(WINDOW)[:, None]
    cw = np.arange(2 * WINDOW)[None, :]
    swa_idx = _bucket_np(WINDOW + rw - cw)
    swa_p = _bias_tiles(rel_bias_table, np.maximum(swa_idx.T, 0))
    swa_s = _bias_tiles(rel_bias_table, np.maximum(swa_idx[:t_len], 0))
    rel_t = _bias_tiles(rel_bias_table, swa_idx[:t_len], rel_far=True, scale=LOG2E)
    a_steps, a_new = _sample_bias_tiles(rel_t, n_heads, t_len)
    bias = {"a_tail": jnp.stack([tail0, tail1], axis=1), "swa_p": swa_p, "swa_s": swa_s,
            "a_steps": a_steps, "a_new": a_new}

    wcast = lambda w: w.astype(BF16)
    weights = (norm_ffn1, norm_attn, norm_ffn2, wcast(w_ffn_in), wcast(w_ffn_out), wcast(w_a_qkv),
               a_q_norm, a_k_norm, a_lambda, a_subln, wcast(w_a_o), norm_kv, wcast(w_b_kv),
               b_k_norm, wcast(w_b_q), b_q_norm, b_sinks, wcast(w_b_o))
    none5 = (None,) * 5
    y_p, ak_p, av_p, bk_p, bv_p = _trunk(x_prompt, False, none5, bias, weights, dims)
    y_s, ak_s, av_s, bk_s, bv_s = _trunk(
        x_sample, True, (cache_a_k, cache_a_v, cache_b_k, cache_b_v, page_table), bias, weights, dims)
    return (y_p, y_s, ak_p, av_p, ak_s, av_s, bk_p, bv_p, bk_s, bv_s)
```

```python
import functools
import math

import numpy as np
import jax
import jax.numpy as jnp
from jax import lax
from jax.experimental import pallas as pl
from jax.experimental.pallas import tpu as pltpu

F32 = jnp.float32
BF16 = jnp.bfloat16

EPS = 1e-6
NEG = -1e30
LOG2E = math.log2(math.e)
N_BUCKETS = 32
MAX_DISTANCE = 128
PAGE_SIZE = 128
WINDOW = 128
LANES = 128
VMEM_LIMIT = 56 * 1024 * 1024

TQ_A = 256
HEADS_A = 4
PAGES_PER_STEP = 4
RING_SLOTS = 3

_NT = (((1,), (1,)), ((), ()))


def _params(sem):
    return pltpu.CompilerParams(dimension_semantics=sem, vmem_limit_bytes=VMEM_LIMIT)


def _bucket_np(dist):
    n = np.maximum(dist, 0)
    max_exact = N_BUCKETS // 2
    nf = np.maximum(n, 1).astype(np.float32)
    large = max_exact + (np.log(nf / np.float32(max_exact))
                         / np.float32(math.log(MAX_DISTANCE / max_exact))
                         * np.float32(N_BUCKETS - max_exact)).astype(np.int32)
    bucket = np.where(n < max_exact, n, np.minimum(large, N_BUCKETS - 1))
    return np.where(dist < 0, -1, bucket).astype(np.int32)


def _bias_kernel(tbl_ref, idx_ref, o_ref, *, rel_far, scale):
    h = pl.program_id(0)
    idx = idx_ref[...]
    acc = jnp.zeros(idx.shape, F32)
    for b in range(N_BUCKETS):
        acc = jnp.where(idx == b, tbl_ref[b, h], acc)
    if rel_far:
        acc = jnp.where(idx < 0, NEG, (acc - tbl_ref[N_BUCKETS - 1, h]) * scale)
    o_ref[0] = acc


def _bias_tiles(table, idx_np, *, rel_far=False, scale=1.0):
    n_heads = table.shape[1]
    r, c = idx_np.shape
    return pl.pallas_call(
        functools.partial(_bias_kernel, rel_far=rel_far, scale=scale),
        grid=(n_heads,),
        in_specs=[pl.BlockSpec(memory_space=pltpu.SMEM),
                  pl.BlockSpec((r, c), lambda h: (0, 0))],
        out_specs=pl.BlockSpec((1, r, c), lambda h: (h, 0, 0)),
        out_shape=jax.ShapeDtypeStruct((n_heads, r, c), F32),
        compiler_params=_params(("arbitrary",)),
        name="bias_tiles",
    )(table, jnp.asarray(idx_np))


def _lam_kernel(al_ref, o_ref, *, lam_init):
    a = al_ref[...]
    s1 = jnp.sum(a[0:1] * a[1:2], axis=-1, keepdims=True)
    s2 = jnp.sum(a[2:3] * a[3:4], axis=-1, keepdims=True)
    lam = jnp.exp(s1) - jnp.exp(s2) + lam_init
    o_ref[...] = jnp.broadcast_to(lam, o_ref.shape)


def _diff_lambda(lam_params, lam_init):
    out = pl.pallas_call(
        functools.partial(_lam_kernel, lam_init=lam_init),
        out_shape=jax.ShapeDtypeStruct((8, LANES), F32),
        name="diff_lambda",
    )(lam_params)
    return out[0, :1]


def _ffn_kernel(x_ref, g_ref, wa_ref, wb_ref, wo_ref, o_ref, xn_ref, acc_ref):
    f = pl.program_id(1)

    @pl.when(f == 0)
    def _():
        xf = x_ref[...]
        y = xf * lax.rsqrt(jnp.mean(xf * xf, axis=-1, keepdims=True) + EPS) * g_ref[...]
        xn_ref[...] = y.astype(BF16)
        acc_ref[...] = jnp.zeros_like(acc_ref)

    xn = xn_ref[...]
    a = jnp.dot(xn, wa_ref[...], preferred_element_type=F32)
    b = jnp.dot(xn, wb_ref[...], preferred_element_type=F32)
    hid = (a * (1.0 / (1.0 + jnp.exp(-a))) * b).astype(BF16)
    acc_ref[...] += jnp.dot(hid, wo_ref[...], preferred_element_type=F32)

    @pl.when(f == pl.num_programs(1) - 1)
    def _():
        o_ref[...] = x_ref[...] + 0.5 * acc_ref[...]


def _ffn(x, g, w_in, w_out, li, k, *, tm, tf):
    m, d = x.shape
    d_ff = w_out.shape[2]
    nf = d_ff // tf
    return pl.pallas_call(
        _ffn_kernel,
        grid=(m // tm, nf),
        in_specs=[pl.BlockSpec((tm, d), lambda i, f: (i, 0)),
                  pl.BlockSpec((1, d), lambda i, f: (0, 0)),
                  pl.BlockSpec((None, None, d, tf), lambda i, f: (li, k, 0, f)),
                  pl.BlockSpec((None, None, d, tf), lambda i, f: (li, k, 0, f + nf)),
                  pl.BlockSpec((None, None, tf, d), lambda i, f: (li, k, f, 0))],
        out_specs=pl.BlockSpec((tm, d), lambda i, f: (i, 0)),
        out_shape=jax.ShapeDtypeStruct((m, d), F32),
        scratch_shapes=[pltpu.VMEM((tm, d), BF16), pltpu.VMEM((tm, d), F32)],
        compiler_params=_params(("parallel", "arbitrary")),
        name="ffn",
    )(x, g.reshape(1, d), w_in, w_in, w_out)


def _proj_kernel(*refs, rms, seg, scale, residual, n_out, t_out):
    refs = list(refs)
    x_ref = refs.pop(0)
    g_ref = refs.pop(0) if rms else None
    w_ref = refs.pop(0)
    res_ref = refs.pop(0) if residual else None
    hg_ref = refs.pop(0) if seg else None
    o_refs = [refs.pop(0) for _ in range(n_out)]
    xn_ref = refs.pop(0)

    @pl.when(pl.program_id(1) == 0)
    def _():
        xf = x_ref[...].astype(F32)
        if rms:
            xf = xf * lax.rsqrt(jnp.mean(xf * xf, axis=-1, keepdims=True) + EPS) * g_ref[...]
        xn_ref[...] = xf.astype(BF16)

    acc = jnp.dot(xn_ref[...], w_ref[...], preferred_element_type=F32)
    tm, tn = acc.shape
    if residual:
        acc = res_ref[...] + acc
    if not seg:
        if t_out:
            t_ref = o_refs.pop()
            t_ref[0] = acc.T.astype(t_ref.dtype)
        for o_ref in o_refs:
            o_ref[...] = acc.astype(o_ref.dtype)
        return
    lane = lax.broadcasted_iota(jnp.int32, (tm, LANES), 1)
    low = lane < seg
    gain = hg_ref[...] * scale
    for cb in range(tn // LANES):
        blk = acc[:, cb * LANES:(cb + 1) * LANES]
        sq = blk * blk
        if seg == LANES:
            ms = jnp.mean(sq, axis=-1, keepdims=True)
        else:
            s_lo = jnp.sum(jnp.where(low, sq, 0.0), axis=-1, keepdims=True)
            s_hi = jnp.sum(jnp.where(low, 0.0, sq), axis=-1, keepdims=True)
            ms = jnp.where(low, s_lo, s_hi) * (1.0 / seg)
        y = blk * lax.rsqrt(ms + EPS) * gain
        for o_ref in o_refs:
            o_ref[:, cb * LANES:(cb + 1) * LANES] = y.astype(o_ref.dtype)


def _proj(x, w, *, n, col_off=0, g=None, head_gain=None, seg=0, scale=1.0, res=None,
          out_dtypes=(F32,), t_rows=0, tm=512, tn=2048):
    m, d = x.shape
    tm = min(tm, m)
    tn = min(tn, n)
    off = col_off // tn
    out_specs = [pl.BlockSpec((tm, tn), lambda i, j: (i, j)) for _ in out_dtypes]
    out_shape = [jax.ShapeDtypeStruct((m, n), dt) for dt in out_dtypes]
    if t_rows:
        assert not seg and t_rows % tm == 0
        per = t_rows // tm
        out_specs[-1] = pl.BlockSpec((1, tn, tm), lambda i, j: (i // per, j, i % per))
        out_shape[-1] = jax.ShapeDtypeStruct((m // t_rows, n, t_rows), out_dtypes[-1])
    ins = [x]
    specs = [pl.BlockSpec((tm, d), lambda i, j: (i, 0))]
    if g is not None:
        ins.append(g.reshape(1, d))
        specs.append(pl.BlockSpec((1, d), lambda i, j: (0, 0)))
    ins.append(w)
    specs.append(pl.BlockSpec((d, tn), lambda i, j: (0, j + off)))
    if res is not None:
        ins.append(res)
        specs.append(pl.BlockSpec((tm, tn), lambda i, j: (i, j)))
    if seg:
        hg = jnp.tile(head_gain.astype(F32), LANES // seg).reshape(1, LANES)
        ins.append(hg)
        specs.append(pl.BlockSpec((1, LANES), lambda i, j: (0, 0)))
    outs = pl.pallas_call(
        functools.partial(_proj_kernel, rms=g is not None, seg=seg, scale=scale,
                          residual=res is not None, n_out=len(out_dtypes), t_out=bool(t_rows)),
        grid=(m // tm, n // tn),
        in_specs=specs,
        out_specs=out_specs,
        out_shape=out_shape,
        scratch_shapes=[pltpu.VMEM((tm, d), BF16)],
        compiler_params=_params(("parallel", "arbitrary")),
        name="proj",
    )(*ins)
    return outs


def _attn_a_prompt_kernel(lam_ref, q_ref, k_ref, vt_ref, bt_ref, sg_ref, o_ref,
                          m_ref, l_ref, acc_ref, *, tq, dqk, n_hd, post_scale):
    i = pl.program_id(2)
    lane = lax.broadcasted_iota(jnp.int32, (tq, LANES), 1)
    qqs = []
    for g in range(n_hd):
        q = q_ref[0, :, g * LANES:(g + 1) * LANES]
        zero = jnp.zeros_like(q)
        qqs.append(jnp.concatenate([jnp.where(lane < dqk, q, zero),
                                    jnp.where(lane < dqk, zero, q)], axis=0))
    m_ref[...] = jnp.full(m_ref.shape, NEG, F32)
    l_ref[...] = jnp.zeros(l_ref.shape, F32)
    acc_ref[...] = jnp.zeros(acc_ref.shape, F32)

    def update(start, size, biases):
        for g in range(n_hd):
            kblk = k_ref[0, pl.ds(start, size), g * LANES:(g + 1) * LANES]
            vtblk = vt_ref[0, g * LANES:(g + 1) * LANES, pl.ds(start, size)]
            s = lax.dot_general(kblk, qqs[g], _NT, preferred_element_type=F32)
            if biases is not None:
                s = s + biases[g]
            m_prev = m_ref[g]
            m_new = jnp.maximum(m_prev, jnp.max(s, axis=0, keepdims=True))
            p = jnp.exp2(s - m_new)
            alpha = jnp.exp2(m_prev - m_new)
            m_ref[g] = m_new
            l_ref[g] = alpha * l_ref[g] + jnp.sum(p, axis=0, keepdims=True)
            acc_ref[g] = alpha * acc_ref[g] + jnp.dot(vtblk, p.astype(BF16),
                                                      preferred_element_type=F32)

    tails = []
    for g in range(n_hd):
        b = bt_ref[g, 0]
        tails.append(jnp.concatenate([b, b], axis=1))
    update(pl.multiple_of(jnp.maximum(i - 1, 0) * tq, tq), 2 * tq, tails)

    def far(f, c):
        update(pl.multiple_of(f * 2 * tq, 2 * tq), 2 * tq, None)
        return c
    lax.fori_loop(0, jnp.maximum(i - 1, 0) // 2, far, 0)

    @pl.when(jnp.logical_and(i >= 2, (i & 1) == 0))
    def _():
        update(pl.multiple_of((i - 2) * tq, tq), tq, None)

    gain = sg_ref[...] * post_scale
    for g in range(n_hd):
        o = acc_ref[g] / l_ref[g]
        d = o[:, :tq] - lam_ref[0] * o[:, tq:]
        y = d * lax.rsqrt(jnp.mean(d * d, axis=0, keepdims=True) + EPS) * gain
        o_ref[0, :, g * LANES:(g + 1) * LANES] = y.T.astype(o_ref.dtype)


def _attn_a_prompt(q, k, vt, lam, bias_tail, subln, post_scale, n_heads):
    bsz, s_len, _ = q.shape
    tq, n_hd = TQ_A, HEADS_A
    gw = n_hd * LANES
    return pl.pallas_call(
        functools.partial(_attn_a_prompt_kernel, tq=tq, dqk=LANES // 2, n_hd=n_hd,
                          post_scale=post_scale),
        grid=(bsz, n_heads // n_hd, s_len // tq),
        in_specs=[pl.BlockSpec(memory_space=pltpu.SMEM),
                  pl.BlockSpec((1, tq, gw), lambda b, h, i: (b, i, h)),
                  pl.BlockSpec((1, s_len, gw), lambda b, h, i: (b, 0, h)),
                  pl.BlockSpec((1, gw, s_len), lambda b, h, i: (b, h, 0)),
                  pl.BlockSpec((n_hd, 1, 2 * tq, tq), lambda b, h, i: (h, jnp.minimum(i, 1), 0, 0)),
                  pl.BlockSpec((LANES, 1), lambda b, h, i: (0, 0))],
        out_specs=pl.BlockSpec((1, tq, gw), lambda b, h, i: (b, i, h)),
        out_shape=jax.ShapeDtypeStruct(q.shape, BF16),
        scratch_shapes=[pltpu.VMEM((n_hd, 1, 2 * tq), F32), pltpu.VMEM((n_hd, 1, 2 * tq), F32),
                        pltpu.VMEM((n_hd, LANES, 2 * tq), F32)],
        compiler_params=_params(("parallel", "parallel", "arbitrary")),
        name="attn_a_prompt",
    )(lam, q, k, vt, bias_tail, subln.reshape(LANES, 1))


def _attn_a_sample_kernel(pt_ref, lam_ref, qbd_ref, k_hbm, v_hbm, kn_ref, vn_ref, bias_ref,
                          bnew_ref, sg_ref, o_ref, kbuf, vbuf, sem, m_ref, l_ref, acc_ref,
                          *, n_pg, n_heads, t_len, page_off, post_scale):
    b = pl.program_id(0)
    j = pl.program_id(1)
    n_steps = pl.num_programs(1)
    n_total = pl.num_programs(0) * n_steps
    g = b * n_steps + j
    slot = g % RING_SLOTS
    rows = 2 * t_len
    cols = n_heads * rows

    def page_copies(step):
        bb = step // n_steps
        jj = step % n_steps
        sl = step % RING_SLOTS
        out = []
        for p in range(n_pg):
            page = pt_ref[bb, jj * n_pg + p] + page_off
            for h in range(n_heads):
                out.append(pltpu.make_async_copy(k_hbm.at[page, :, h, :], kbuf.at[sl, p, h],
                                                 sem.at[sl]))
                out.append(pltpu.make_async_copy(v_hbm.at[page, :, h, :], vbuf.at[sl, p, h],
                                                 sem.at[sl]))
        return out

    @pl.when(g == 0)
    def _():
        for ahead in range(RING_SLOTS - 1):
            for c in page_copies(g + ahead):
                c.start()

    @pl.when(g + RING_SLOTS - 1 < n_total)
    def _():
        for c in page_copies(g + RING_SLOTS - 1):
            c.start()

    @pl.when(j == 0)
    def _():
        m_ref[...] = jnp.full(m_ref.shape, NEG, F32)
        l_ref[...] = jnp.zeros(l_ref.shape, F32)
        acc_ref[...] = jnp.zeros(acc_ref.shape, F32)

    for c in page_copies(g):
        c.wait()

    def process(k_tiles, v_tiles, bias):
        n_p = len(k_tiles)
        kk = jnp.concatenate(
            [jnp.concatenate([t().astype(BF16) for t in k_tiles[p]], axis=1) for p in range(n_p)],
            axis=0)
        s = jnp.dot(kk, qbd_ref[0], preferred_element_type=F32) + bias
        m_prev = m_ref[...]
        m_new = jnp.maximum(m_prev, jnp.max(s, axis=0, keepdims=True))
        p_t = jnp.exp2(s - m_new)
        alpha = jnp.exp2(m_prev - m_new)
        m_ref[...] = m_new
        l_ref[...] = alpha * l_ref[...] + jnp.sum(p_t, axis=0, keepdims=True)
        pb = p_t.T.astype(BF16)
        heads = []
        for h in range(n_heads):
            vh = jnp.concatenate([v_tiles[p][h]().astype(BF16) for p in range(n_p)], axis=0)
            heads.append(jnp.dot(pb[h * rows:(h + 1) * rows], vh, preferred_element_type=F32))
        alpha_col = jnp.broadcast_to(alpha, (8, cols)).T[:, :1]
        acc_ref[...] = alpha_col * acc_ref[...] + jnp.concatenate(heads, axis=0)

    k_tiles = [[(lambda p=p, h=h: kbuf[slot, p, h]) for h in range(n_heads)] for p in range(n_pg)]
    v_tiles = [[(lambda p=p, h=h: vbuf[slot, p, h]) for h in range(n_heads)] for p in range(n_pg)]
    process(k_tiles, v_tiles, bias_ref[0])

    @pl.when(j == n_steps - 1)
    def _():
        kn = [[(lambda h=h: kn_ref[0, :, h * LANES:(h + 1) * LANES]) for h in range(n_heads)]]
        vn = [[(lambda h=h: vn_ref[0, :, h * LANES:(h + 1) * LANES]) for h in range(n_heads)]]
        process(kn, vn, bnew_ref[...])
        l_col = jnp.broadcast_to(l_ref[...], (8, cols)).T[:, :1]
        o = acc_ref[...] / l_col
        gain = sg_ref[...] * post_scale
        for h in range(n_heads):
            d = o[h * rows:h * rows + t_len] - lam_ref[0] * o[h * rows + t_len:(h + 1) * rows]
            y = d * lax.rsqrt(jnp.mean(d * d, axis=-1, keepdims=True) + EPS) * gain
            o_ref[0, :, h * LANES:(h + 1) * LANES] = y


def _attn_a_sample(qbd, cache_k, cache_v, k_new, v_new, page_table, page_off, lam,
                   bias_steps, bias_new, subln, post_scale, n_heads, t_len):
    bsz, n_pages = page_table.shape
    n_pg = PAGES_PER_STEP
    n_steps = n_pages // n_pg
    d, cols = qbd.shape[1:]
    hd = d // n_heads
    grid_spec = pltpu.PrefetchScalarGridSpec(
        num_scalar_prefetch=1,
        grid=(bsz, n_steps),
        in_specs=[pl.BlockSpec(memory_space=pltpu.SMEM),
                  pl.BlockSpec((1, d, cols), lambda b, j, pt: (b, 0, 0)),
                  pl.BlockSpec(memory_space=pl.ANY),
                  pl.BlockSpec(memory_space=pl.ANY),
                  pl.BlockSpec((1, PAGE_SIZE, d), lambda b, j, pt: (b, 0, 0)),
                  pl.BlockSpec((1, PAGE_SIZE, d), lambda b, j, pt: (b, 0, 0)),
                  pl.BlockSpec((1, n_pg * PAGE_SIZE, cols),
                               lambda b, j, pt: (jnp.where(j == n_steps - 1, 1, 0), 0, 0)),
                  pl.BlockSpec((PAGE_SIZE, cols), lambda b, j, pt: (0, 0)),
                  pl.BlockSpec((1, LANES), lambda b, j, pt: (0, 0))],
        out_specs=pl.BlockSpec((1, t_len, d), lambda b, j, pt: (b, 0, 0)),
        scratch_shapes=[pltpu.VMEM((RING_SLOTS, n_pg, n_heads, PAGE_SIZE, hd), F32),
                        pltpu.VMEM((RING_SLOTS, n_pg, n_heads, PAGE_SIZE, hd), F32),
                        pltpu.SemaphoreType.DMA((RING_SLOTS,)),
                        pltpu.VMEM((1, cols), F32), pltpu.VMEM((1, cols), F32),
                        pltpu.VMEM((cols, hd), F32)],
    )
    return pl.pallas_call(
        functools.partial(_attn_a_sample_kernel, n_pg=n_pg, n_heads=n_heads, t_len=t_len,
                          page_off=page_off, post_scale=post_scale),
        grid_spec=grid_spec,
        out_shape=jax.ShapeDtypeStruct((bsz, t_len, d), F32),
        compiler_params=_params(("arbitrary", "arbitrary")),
        name="attn_a_sample",
    )(page_table, lam, qbd, cache_k, cache_v, k_new, v_new, bias_steps, bias_new,
      subln.reshape(1, LANES))


def _sink_softmax(s, sink_col):
    m = jnp.maximum(jnp.max(s, axis=-1, keepdims=True), sink_col)
    e = jnp.exp(s - m)
    denom = jnp.sum(e, axis=-1, keepdims=True) + jnp.exp(sink_col - m)
    return (e / denom).astype(BF16)


def _swa_prompt_kernel(sink_ref, q_ref, kp_ref, kc_ref, vtp_ref, vtc_ref, bias_ref, o_ref,
                       *, kv_heads, group):
    n = pl.program_id(1)
    w = WINDOW
    key = lax.broadcasted_iota(jnp.int32, (2 * w, group * w), 0)
    qry = lax.broadcasted_iota(jnp.int32, (2 * w, group * w), 1) & (w - 1)
    dist = w + qry - key
    first_key = jnp.where(n > 0, 0, w)
    valid = (dist >= 0) & (dist < w) & (key >= first_key)
    for kv in range(kv_heads):
        hs = [kv * group + g for g in range(group)]
        q = jnp.concatenate([q_ref[0, :, h * LANES:(h + 1) * LANES] for h in hs], axis=0)
        ksl = slice(kv * LANES, (kv + 1) * LANES)
        kcat = jnp.concatenate([kp_ref[0, :, ksl], kc_ref[0, :, ksl]], axis=0)
        vtcat = jnp.concatenate([vtp_ref[0, ksl, :], vtc_ref[0, ksl, :]], axis=1)
        s = lax.dot_general(kcat, q, _NT, preferred_element_type=F32)
        s = s + jnp.concatenate([bias_ref[h] for h in hs], axis=1)
        s = jnp.where(valid, s, NEG)
        sink = jnp.concatenate([jnp.full((1, w), sink_ref[h], F32) for h in hs], axis=1)
        m = jnp.maximum(jnp.max(s, axis=0, keepdims=True), sink)
        e = jnp.exp(s - m)
        denom = jnp.sum(e, axis=0, keepdims=True) + jnp.exp(sink - m)
        o_t = jnp.dot(vtcat, (e / denom).astype(BF16), preferred_element_type=F32)
        for g, h in enumerate(hs):
            o_ref[0, :, h * LANES:(h + 1) * LANES] = o_t[:, g * w:(g + 1) * w].T.astype(o_ref.dtype)


def _swa_prompt(q, k, vt, sinks, bias, kv_heads, group):
    bsz, s_len, d = q.shape
    nb = s_len // WINDOW
    kw = kv_heads * LANES
    prev = lambda b, n: (b, jnp.maximum(n - 1, 0), 0)
    cur = lambda b, n: (b, n, 0)
    prev_t = lambda b, n: (b, 0, jnp.maximum(n - 1, 0))
    cur_t = lambda b, n: (b, 0, n)
    return pl.pallas_call(
        functools.partial(_swa_prompt_kernel, kv_heads=kv_heads, group=group),
        grid=(bsz, nb),
        in_specs=[pl.BlockSpec(memory_space=pltpu.SMEM),
                  pl.BlockSpec((1, WINDOW, d), cur),
                  pl.BlockSpec((1, WINDOW, kw), prev),
                  pl.BlockSpec((1, WINDOW, kw), cur),
                  pl.BlockSpec((1, kw, WINDOW), prev_t),
                  pl.BlockSpec((1, kw, WINDOW), cur_t),
                  pl.BlockSpec(bias.shape, lambda b, n: (0, 0, 0))],
        out_specs=pl.BlockSpec((1, WINDOW, d), cur),
        out_shape=jax.ShapeDtypeStruct(q.shape, BF16),
        compiler_params=_params(("parallel", "arbitrary")),
        name="swa_prompt",
    )(sinks, q, k, k, vt, vt, bias)


def _swa_sample_kernel(sink_ref, q_ref, k_ref, v_ref, bias_ref, o_ref, *, kv_heads, group, t_len):
    w = WINDOW
    for kv in range(kv_heads):
        q = jnp.concatenate(
            [q_ref[0, :, (kv * group + g) * LANES:(kv * group + g + 1) * LANES]
             for g in range(group)], axis=0).astype(BF16)
        kk = k_ref[0, :, kv * LANES:(kv + 1) * LANES]
        vv = v_ref[0, :, kv * LANES:(kv + 1) * LANES]
        s = lax.dot_general(q, kk, _NT, preferred_element_type=F32)
        s = s + bias_ref[kv * group:(kv + 1) * group].reshape(group * t_len, 2 * w)
        row = lax.broadcasted_iota(jnp.int32, s.shape, 0) & (t_len - 1)
        col = lax.broadcasted_iota(jnp.int32, s.shape, 1)
        dist = w + row - col
        s = jnp.where((dist >= 0) & (dist < w), s, NEG)
        sink_col = jnp.concatenate(
            [jnp.full((t_len, 1), sink_ref[kv * group + g], F32) for g in range(group)], axis=0)
        o = jnp.dot(_sink_softmax(s, sink_col), vv, preferred_element_type=F32)
        for g in range(group):
            hh = kv * group + g
            o_ref[0, :, hh * LANES:(hh + 1) * LANES] = o[g * t_len:(g + 1) * t_len]


def _swa_sample(q, kk, vv, sinks, bias, kv_heads, group):
    bsz, t_len, d = q.shape
    n_keys = kk.shape[1]
    return pl.pallas_call(
        functools.partial(_swa_sample_kernel, kv_heads=kv_heads, group=group, t_len=t_len),
        grid=(bsz,),
        in_specs=[pl.BlockSpec(memory_space=pltpu.SMEM),
                  pl.BlockSpec((1, t_len, d), lambda b: (b, 0, 0)),
                  pl.BlockSpec((1, n_keys, kv_heads * LANES), lambda b: (b, 0, 0)),
                  pl.BlockSpec((1, n_keys, kv_heads * LANES), lambda b: (b, 0, 0)),
                  pl.BlockSpec(bias.shape, lambda b: (0, 0, 0))],
        out_specs=pl.BlockSpec((1, t_len, d), lambda b: (b, 0, 0)),
        out_shape=jax.ShapeDtypeStruct(q.shape, F32),
        compiler_params=_params(("parallel",)),
        name="swa_sample",
    )(sinks, q, kk, vv, bias)


def _sample_bias_tiles(rel_t, n_heads, t_len):
    cols = n_heads * 2 * t_len

    def tile(per_key):
        n_keys = per_key.shape[-1]
        t = jnp.transpose(per_key, (2, 0, 1))[:, :, None, :]
        return jnp.broadcast_to(t, (n_keys, n_heads, 2, t_len)).reshape(n_keys, cols)

    last = tile(rel_t[:, :, :PAGE_SIZE])
    zero = jnp.zeros(((PAGES_PER_STEP - 1) * PAGE_SIZE, cols), F32)
    steps = jnp.stack([jnp.zeros((PAGES_PER_STEP * PAGE_SIZE, cols), F32),
                       jnp.concatenate([zero, last], axis=0)])
    new = tile(rel_t[:, :, PAGE_SIZE:PAGE_SIZE + t_len])
    new = jnp.concatenate([new, jnp.full((PAGE_SIZE - t_len, cols), NEG, F32)], axis=0)
    return steps, new


def _block_diag_queries(q, n_heads, dqk):
    bsz, t_len, _ = q.shape
    qt = jnp.transpose(q.reshape(bsz, t_len, n_heads, 2, dqk), (0, 2, 3, 4, 1))
    comp = jnp.eye(2, dtype=q.dtype)
    blk = jnp.einsum("bhcdt,ce->bhcdet", qt, comp).reshape(bsz, n_heads, 2 * dqk, 2 * t_len)
    head = jnp.eye(n_heads, dtype=q.dtype)
    full = jnp.einsum("bhdk,hg->bhdgk", blk, head)
    return full.reshape(bsz, n_heads * 2 * dqk, n_heads * 2 * t_len)


def _trunk(x, sample, caches, bias, weights, dims):
    (cache_a_k, cache_a_v, cache_b_k, cache_b_v, page_table) = caches
    (norm_ffn1, norm_attn, norm_ffn2, w_ffn_in, w_ffn_out, w_a_qkv, a_q_norm, a_k_norm,
     a_lambda, a_subln, w_a_o, norm_kv, w_b_kv, b_k_norm, w_b_q, b_q_norm, b_sinks, w_b_o) = weights
    depth, n_a, n_heads, kv_heads = dims
    bsz, t_len, d = x.shape
    m = bsz * t_len
    group = n_heads // kv_heads
    dqk = d // n_heads // 2
    hd = d // n_heads
    kv_w = kv_heads * hd
    tm = min(512, m)
    x = x.reshape(m, d)
    a_ks, a_vs = [], []
    kv_k = kv_v = kb16 = vb16 = None
    for li in range(depth):
        if li == n_a:
            kv_k, kb16 = _proj(x, w_b_kv, n=kv_w, g=norm_kv, head_gain=b_k_norm, seg=hd,
                               out_dtypes=(F32, BF16), tm=tm)
            kv_v, vb16 = _proj(x, w_b_kv, n=kv_w, col_off=kv_w, g=norm_kv,
                               out_dtypes=(F32, BF16), t_rows=0 if sample else t_len, tm=tm)
        x = _ffn(x, norm_ffn1[li], w_ffn_in, w_ffn_out, li, 0, tm=tm, tf=512)
        q_dt = F32 if sample else BF16
        if li < n_a:
            lam_init = 0.8 - 0.6 * math.exp(-0.3 * li)
            lam = _diff_lambda(a_lambda[li], lam_init)
            (q,) = _proj(x, w_a_qkv[li], n=d, g=norm_attn[li], head_gain=a_q_norm[li], seg=dqk,
                         scale=dqk ** -0.5 * LOG2E, out_dtypes=(q_dt,), tm=tm)
            k, k16 = _proj(x, w_a_qkv[li], n=d, col_off=d, g=norm_attn[li],
                           head_gain=a_k_norm[li], seg=dqk, out_dtypes=(F32, BF16), tm=tm)
            v, v16 = _proj(x, w_a_qkv[li], n=d, col_off=2 * d, g=norm_attn[li],
                           out_dtypes=(F32, BF16), t_rows=0 if sample else t_len, tm=tm)
            a_ks.append(k.reshape(bsz, t_len, n_heads, hd))
            a_vs.append(v.reshape(bsz, t_len, n_heads, hd))
            if sample:
                n_pool = cache_a_k.shape[1]
                qbd = _block_diag_queries(q.reshape(bsz, t_len, d), n_heads, dqk).astype(BF16)
                pad = ((0, 0), (0, PAGE_SIZE - t_len), (0, 0))
                o = _attn_a_sample(
                    qbd,
                    cache_a_k.reshape((-1,) + cache_a_k.shape[2:]),
                    cache_a_v.reshape((-1,) + cache_a_v.shape[2:]),
                    jnp.pad(k.reshape(bsz, t_len, d), pad), jnp.pad(v.reshape(bsz, t_len, d), pad),
                    page_table, li * n_pool, lam, bias["a_steps"], bias["a_new"],
                    a_subln[li], 1.0 - lam_init, n_heads, t_len)
            else:
                o = _attn_a_prompt(q.reshape(bsz, t_len, d), k16.reshape(bsz, t_len, d), v16, lam,
                                   bias["a_tail"], a_subln[li], 1.0 - lam_init, n_heads)
            (x,) = _proj(o.reshape(m, d), w_a_o[li], n=d, res=x, tm=tm)
        else:
            bi = li - n_a
            (q,) = _proj(x, w_b_q[bi], n=d, g=norm_attn[li], head_gain=b_q_norm[bi], seg=hd,
                         scale=hd ** -0.5, out_dtypes=(q_dt,), tm=tm)
            if sample:
                zpad = jnp.zeros((bsz, WINDOW - t_len, kv_w), BF16)
                kk = jnp.concatenate([cache_b_k.reshape(bsz, WINDOW, kv_w).astype(BF16),
                                      kb16.reshape(bsz, t_len, kv_w), zpad], axis=1)
                vv = jnp.concatenate([cache_b_v.reshape(bsz, WINDOW, kv_w).astype(BF16),
                                      vb16.reshape(bsz, t_len, kv_w), zpad], axis=1)
                o = _swa_sample(q.reshape(bsz, t_len, d), kk, vv, b_sinks[bi], bias["swa_s"],
                                kv_heads, group)
            else:
                o = _swa_prompt(q.reshape(bsz, t_len, d), kb16.reshape(bsz, t_len, kv_w), vb16,
                                b_sinks[bi], bias["swa_p"], kv_heads, group)
            (x,) = _proj(o.reshape(m, d), w_b_o[bi], n=d, res=x, tm=tm)
        x = _ffn(x, norm_ffn2[li], w_ffn_in, w_ffn_out, li, 1, tm=tm, tf=512)
    kv_k = kv_k.reshape(bsz, t_len, kv_heads, hd)
    kv_v = kv_v.reshape(bsz, t_len, kv_heads, hd)
    if sample:
        kv_k = jnp.concatenate([cache_b_k, kv_k], axis=1)
        kv_v = jnp.concatenate([cache_b_v, kv_v], axis=1)
    keep = min(WINDOW, kv_k.shape[1])
    return (x.reshape(bsz, t_len, d), jnp.stack(a_ks), jnp.stack(a_vs),
            kv_k[:, -keep:], kv_v[:, -keep:])


def kernel(x_prompt, x_sample, cache_a_k, cache_a_v, cache_b_k, cache_b_v, page_table, rel_bias_table, norm_ffn1, norm_attn, norm_ffn2, w_ffn_in, w_ffn_out, w_a_qkv, a_q_norm, a_k_norm, a_lambda, a_subln, w_a_o, norm_kv, w_b_kv, b_k_norm, w_b_q, b_q_norm, b_sinks, w_b_o):
    depth = norm_ffn1.shape[0]
    n_a = w_a_qkv.shape[0]
    n_heads = rel_bias_table.shape[1]
    kv_heads = cache_b_k.shape[2]
    t_len = x_sample.shape[1]
    dims = (depth, n_a, n_heads, kv_heads)

    tq = TQ_A
    kc = np.arange(2 * tq)[:, None]
    qr = np.arange(tq)[None, :]
    tail0 = _bias_tiles(rel_bias_table, _bucket_np(qr - kc), rel_far=True, scale=LOG2E)
    tail1 = _bias_tiles(rel_bias_table, _bucket_np(tq + qr - kc), rel_far=True, scale=LOG2E)
    rw = np.arange(WINDOW)[:, None]
    cw = np.arange(2 * WINDOW)[None, :]
    swa_idx = _bucket_np(WINDOW + rw - cw)
    swa_p = _bias_tiles(rel_bias_table, np.maximum(swa_idx.T, 0))
    swa_s = _bias_tiles(rel_bias_table, np.maximum(swa_idx[:t_len], 0))
    rel_t = _bias_tiles(rel_bias_table, swa_idx[:t_len], rel_far=True, scale=LOG2E)
    a_steps, a_new = _sample_bias_tiles(rel_t, n_heads, t_len)
    bias = {"a_tail": jnp.stack([tail0, tail1], axis=1), "swa_p": swa_p, "swa_s": swa_s,
            "a_steps": a_steps, "a_new": a_new}

    wcast = lambda w: w.astype(BF16)
    weights = (norm_ffn1, norm_attn, norm_ffn2, wcast(w_ffn_in), wcast(w_ffn_out), wcast(w_a_qkv),
               a_q_norm, a_k_norm, a_lambda, a_subln, wcast(w_a_o), norm_kv, wcast(w_b_kv),
               b_k_norm, wcast(w_b_q), b_q_norm, b_sinks, wcast(w_b_o))
    none5 = (None,) * 5
    y_p, ak_p, av_p, bk_p, bv_p = _trunk(x_prompt, False, none5, bias, weights, dims)
    y_s, ak_s, av_s, bk_s, bv_s = _trunk(
        x_sample, True, (cache_a_k, cache_a_v, cache_b_k, cache_b_v, page_table), bias, weights, dims)
    return (y_p, y_s, ak_p, av_p, ak_s, av_s, bk_p, bv_p, bk_s, bv_s)
```

```python
import functools
import math

import numpy as np
import jax
import jax.numpy as jnp
from jax import lax
from jax.experimental import pallas as pl
from jax.experimental.pallas import tpu as pltpu

F32 = jnp.float32
BF16 = jnp.bfloat16

EPS = 1e-6
NEG = -1e30
LOG2E = math.log2(math.e)
N_BUCKETS = 32
MAX_DISTANCE = 128
PAGE_SIZE = 128
WINDOW = 128
LANES = 128
VMEM_LIMIT = 56 * 1024 * 1024

TQ_A = 256
HEADS_A = 8
PAGES_PER_STEP = 4
RING_SLOTS = 3

_NT = (((1,), (1,)), ((), ()))


def _params(sem):
    return pltpu.CompilerParams(dimension_semantics=sem, vmem_limit_bytes=VMEM_LIMIT)


def _bucket_np(dist):
    n = np.maximum(dist, 0)
    max_exact = N_BUCKETS // 2
    nf = np.maximum(n, 1).astype(np.float32)
    large = max_exact + (np.log(nf / np.float32(max_exact))
                         / np.float32(math.log(MAX_DISTANCE / max_exact))
                         * np.float32(N_BUCKETS - max_exact)).astype(np.int32)
    bucket = np.where(n < max_exact, n, np.minimum(large, N_BUCKETS - 1))
    return np.where(dist < 0, -1, bucket).astype(np.int32)


def _bias_kernel(tbl_ref, idx_ref, o_ref, *, rel_far, scale):
    h = pl.program_id(0)
    idx = idx_ref[...]
    acc = jnp.zeros(idx.shape, F32)
    for b in range(N_BUCKETS):
        acc = jnp.where(idx == b, tbl_ref[b, h], acc)
    if rel_far:
        acc = jnp.where(idx < 0, NEG, (acc - tbl_ref[N_BUCKETS - 1, h]) * scale)
    o_ref[0] = acc


def _bias_tiles(table, idx_np, *, rel_far=False, scale=1.0):
    n_heads = table.shape[1]
    r, c = idx_np.shape
    return pl.pallas_call(
        functools.partial(_bias_kernel, rel_far=rel_far, scale=scale),
        grid=(n_heads,),
        in_specs=[pl.BlockSpec(memory_space=pltpu.SMEM),
                  pl.BlockSpec((r, c), lambda h: (0, 0))],
        out_specs=pl.BlockSpec((1, r, c), lambda h: (h, 0, 0)),
        out_shape=jax.ShapeDtypeStruct((n_heads, r, c), F32),
        compiler_params=_params(("arbitrary",)),
        name="bias_tiles",
    )(table, jnp.asarray(idx_np))


def _lam_kernel(al_ref, o_ref, *, lam_init):
    a = al_ref[...]
    s1 = jnp.sum(a[0:1] * a[1:2], axis=-1, keepdims=True)
    s2 = jnp.sum(a[2:3] * a[3:4], axis=-1, keepdims=True)
    lam = jnp.exp(s1) - jnp.exp(s2) + lam_init
    o_ref[...] = jnp.broadcast_to(lam, o_ref.shape)


def _diff_lambda(lam_params, lam_init):
    out = pl.pallas_call(
        functools.partial(_lam_kernel, lam_init=lam_init),
        out_shape=jax.ShapeDtypeStruct((8, LANES), F32),
        name="diff_lambda",
    )(lam_params)
    return out[0, :1]


def _ffn_kernel(x_ref, g_ref, wa_ref, wb_ref, wo_ref, o_ref, xn_ref, acc_ref):
    f = pl.program_id(1)

    @pl.when(f == 0)
    def _():
        xf = x_ref[...]
        y = xf * lax.rsqrt(jnp.mean(xf * xf, axis=-1, keepdims=True) + EPS) * g_ref[...]
        xn_ref[...] = y.astype(BF16)
        acc_ref[...] = jnp.zeros_like(acc_ref)

    xn = xn_ref[...]
    a = jnp.dot(xn, wa_ref[...], preferred_element_type=F32)
    b = jnp.dot(xn, wb_ref[...], preferred_element_type=F32)
    hid = (a * (1.0 / (1.0 + jnp.exp(-a))) * b).astype(BF16)
    acc_ref[...] += jnp.dot(hid, wo_ref[...], preferred_element_type=F32)

    @pl.when(f == pl.num_programs(1) - 1)
    def _():
        o_ref[...] = x_ref[...] + 0.5 * acc_ref[...]


def _ffn(x, g, w_in, w_out, li, k, *, tm, tf):
    m, d = x.shape
    d_ff = w_out.shape[2]
    nf = d_ff // tf
    return pl.pallas_call(
        _ffn_kernel,
        grid=(m // tm, nf),
        in_specs=[pl.BlockSpec((tm, d), lambda i, f: (i, 0)),
                  pl.BlockSpec((1, d), lambda i, f: (0, 0)),
                  pl.BlockSpec((None, None, d, tf), lambda i, f: (li, k, 0, f)),
                  pl.BlockSpec((None, None, d, tf), lambda i, f: (li, k, 0, f + nf)),
                  pl.BlockSpec((None, None, tf, d), lambda i, f: (li, k, f, 0))],
        out_specs=pl.BlockSpec((tm, d), lambda i, f: (i, 0)),
        out_shape=jax.ShapeDtypeStruct((m, d), F32),
        scratch_shapes=[pltpu.VMEM((tm, d), BF16), pltpu.VMEM((tm, d), F32)],
        compiler_params=_params(("parallel", "arbitrary")),
        name="ffn",
    )(x, g.reshape(1, d), w_in, w_in, w_out)


def _proj_kernel(*refs, rms, seg, scale, residual, n_out, t_out):
    refs = list(refs)
    x_ref = refs.pop(0)
    g_ref = refs.pop(0) if rms else None
    w_ref = refs.pop(0)
    res_ref = refs.pop(0) if residual else None
    hg_ref = refs.pop(0) if seg else None
    o_refs = [refs.pop(0) for _ in range(n_out)]
    xn_ref = refs.pop(0)

    @pl.when(pl.program_id(1) == 0)
    def _():
        xf = x_ref[...].astype(F32)
        if rms:
            xf = xf * lax.rsqrt(jnp.mean(xf * xf, axis=-1, keepdims=True) + EPS) * g_ref[...]
        xn_ref[...] = xf.astype(BF16)

    acc = jnp.dot(xn_ref[...], w_ref[...], preferred_element_type=F32)
    tm, tn = acc.shape
    if residual:
        acc = res_ref[...] + acc
    if not seg:
        if t_out:
            t_ref = o_refs.pop()
            t_ref[0] = acc.T.astype(t_ref.dtype)
        for o_ref in o_refs:
            o_ref[...] = acc.astype(o_ref.dtype)
        return
    lane = lax.broadcasted_iota(jnp.int32, (tm, LANES), 1)
    low = lane < seg
    gain = hg_ref[...] * scale
    for cb in range(tn // LANES):
        blk = acc[:, cb * LANES:(cb + 1) * LANES]
        sq = blk * blk
        if seg == LANES:
            ms = jnp.mean(sq, axis=-1, keepdims=True)
        else:
            s_lo = jnp.sum(jnp.where(low, sq, 0.0), axis=-1, keepdims=True)
            s_hi = jnp.sum(jnp.where(low, 0.0, sq), axis=-1, keepdims=True)
            ms = jnp.where(low, s_lo, s_hi) * (1.0 / seg)
        y = blk * lax.rsqrt(ms + EPS) * gain
        for o_ref in o_refs:
            o_ref[:, cb * LANES:(cb + 1) * LANES] = y.astype(o_ref.dtype)


def _proj(x, w, *, n, col_off=0, g=None, head_gain=None, seg=0, scale=1.0, res=None,
          out_dtypes=(F32,), t_rows=0, tm=512, tn=2048):
    m, d = x.shape
    tm = min(tm, m)
    tn = min(tn, n)
    off = col_off // tn
    out_specs = [pl.BlockSpec((tm, tn), lambda i, j: (i, j)) for _ in out_dtypes]
    out_shape = [jax.ShapeDtypeStruct((m, n), dt) for dt in out_dtypes]
    if t_rows:
        assert not seg and t_rows % tm == 0
        per = t_rows // tm
        out_specs[-1] = pl.BlockSpec((1, tn, tm), lambda i, j: (i // per, j, i % per))
        out_shape[-1] = jax.ShapeDtypeStruct((m // t_rows, n, t_rows), out_dtypes[-1])
    ins = [x]
    specs = [pl.BlockSpec((tm, d), lambda i, j: (i, 0))]
    if g is not None:
        ins.append(g.reshape(1, d))
        specs.append(pl.BlockSpec((1, d), lambda i, j: (0, 0)))
    ins.append(w)
    specs.append(pl.BlockSpec((d, tn), lambda i, j: (0, j + off)))
    if res is not None:
        ins.append(res)
        specs.append(pl.BlockSpec((tm, tn), lambda i, j: (i, j)))
    if seg:
        hg = jnp.tile(head_gain.astype(F32), LANES // seg).reshape(1, LANES)
        ins.append(hg)
        specs.append(pl.BlockSpec((1, LANES), lambda i, j: (0, 0)))
    outs = pl.pallas_call(
        functools.partial(_proj_kernel, rms=g is not None, seg=seg, scale=scale,
                          residual=res is not None, n_out=len(out_dtypes), t_out=bool(t_rows)),
        grid=(m // tm, n // tn),
        in_specs=specs,
        out_specs=out_specs,
        out_shape=out_shape,
        scratch_shapes=[pltpu.VMEM((tm, d), BF16)],
        compiler_params=_params(("parallel", "arbitrary")),
        name="proj",
    )(*ins)
    return outs


def _attn_a_prompt_kernel(lam_ref, q_ref, k_ref, vt_ref, bt_ref, sg_ref, o_ref,
                          m_ref, l_ref, acc_ref, *, tq, dqk, n_hd, post_scale):
    i = pl.program_id(2)
    lane = lax.broadcasted_iota(jnp.int32, (tq, LANES), 1)
    qqs = []
    for g in range(n_hd):
        q = q_ref[0, :, g * LANES:(g + 1) * LANES]
        zero = jnp.zeros_like(q)
        qqs.append(jnp.concatenate([jnp.where(lane < dqk, q, zero),
                                    jnp.where(lane < dqk, zero, q)], axis=0))
    m_ref[...] = jnp.full(m_ref.shape, NEG, F32)
    l_ref[...] = jnp.zeros(l_ref.shape, F32)
    acc_ref[...] = jnp.zeros(acc_ref.shape, F32)

    def update(start, size, biases):
        for g in range(n_hd):
            kblk = k_ref[0, pl.ds(start, size), g * LANES:(g + 1) * LANES]
            vtblk = vt_ref[0, g * LANES:(g + 1) * LANES, pl.ds(start, size)]
            s = lax.dot_general(kblk, qqs[g], _NT, preferred_element_type=F32)
            if biases is not None:
                s = s + biases[g]
            m_prev = m_ref[g]
            m_new = jnp.maximum(m_prev, jnp.max(s, axis=0, keepdims=True))
            p = jnp.exp2(s - m_new)
            alpha = jnp.exp2(m_prev - m_new)
            m_ref[g] = m_new
            l_ref[g] = alpha * l_ref[g] + jnp.sum(p, axis=0, keepdims=True)
            acc_ref[g] = alpha * acc_ref[g] + jnp.dot(vtblk, p.astype(BF16),
                                                      preferred_element_type=F32)

    tails = []
    for g in range(n_hd):
        b = bt_ref[g, 0]
        tails.append(jnp.concatenate([b, b], axis=1))
    update(pl.multiple_of(jnp.maximum(i - 1, 0) * tq, tq), 2 * tq, tails)

    def far(f, c):
        update(pl.multiple_of(f * 2 * tq, 2 * tq), 2 * tq, None)
        return c
    lax.fori_loop(0, jnp.maximum(i - 1, 0) // 2, far, 0)

    @pl.when(jnp.logical_and(i >= 2, (i & 1) == 0))
    def _():
        update(pl.multiple_of((i - 2) * tq, tq), tq, None)

    gain = sg_ref[...] * post_scale
    for g in range(n_hd):
        o = acc_ref[g] / l_ref[g]
        d = o[:, :tq] - lam_ref[0] * o[:, tq:]
        y = d * lax.rsqrt(jnp.mean(d * d, axis=0, keepdims=True) + EPS) * gain
        o_ref[0, :, g * LANES:(g + 1) * LANES] = y.T.astype(o_ref.dtype)


def _attn_a_prompt(q, k, vt, lam, bias_tail, subln, post_scale, n_heads):
    bsz, s_len, _ = q.shape
    tq, n_hd = TQ_A, HEADS_A
    gw = n_hd * LANES
    return pl.pallas_call(
        functools.partial(_attn_a_prompt_kernel, tq=tq, dqk=LANES // 2, n_hd=n_hd,
                          post_scale=post_scale),
        grid=(bsz, n_heads // n_hd, s_len // tq),
        in_specs=[pl.BlockSpec(memory_space=pltpu.SMEM),
                  pl.BlockSpec((1, tq, gw), lambda b, h, i: (b, i, h)),
                  pl.BlockSpec((1, s_len, gw), lambda b, h, i: (b, 0, h)),
                  pl.BlockSpec((1, gw, s_len), lambda b, h, i: (b, h, 0)),
                  pl.BlockSpec((n_hd, 1, 2 * tq, tq), lambda b, h, i: (h, jnp.minimum(i, 1), 0, 0)),
                  pl.BlockSpec((LANES, 1), lambda b, h, i: (0, 0))],
        out_specs=pl.BlockSpec((1, tq, gw), lambda b, h, i: (b, i, h)),
        out_shape=jax.ShapeDtypeStruct(q.shape, BF16),
        scratch_shapes=[pltpu.VMEM((n_hd, 1, 2 * tq), F32), pltpu.VMEM((n_hd, 1, 2 * tq), F32),
                        pltpu.VMEM((n_hd, LANES, 2 * tq), F32)],
        compiler_params=_params(("parallel", "parallel", "arbitrary")),
        name="attn_a_prompt",
    )(lam, q, k, vt, bias_tail, subln.reshape(LANES, 1))


def _attn_a_sample_kernel(pt_ref, lam_ref, qbd_ref, k_hbm, v_hbm, kn_ref, vn_ref, bias_ref,
                          bnew_ref, sg_ref, o_ref, kbuf, vbuf, sem, m_ref, l_ref, acc_ref,
                          *, n_pg, n_heads, t_len, page_off, post_scale):
    b = pl.program_id(0)
    j = pl.program_id(1)
    n_steps = pl.num_programs(1)
    n_total = pl.num_programs(0) * n_steps
    g = b * n_steps + j
    slot = g % RING_SLOTS
    rows = 2 * t_len
    cols = n_heads * rows

    def page_copies(step):
        bb = step // n_steps
        jj = step % n_steps
        sl = step % RING_SLOTS
        out = []
        for p in range(n_pg):
            page = pt_ref[bb, jj * n_pg + p] + page_off
            for h in range(n_heads):
                out.append(pltpu.make_async_copy(k_hbm.at[page, :, h, :], kbuf.at[sl, p, h],
                                                 sem.at[sl]))
                out.append(pltpu.make_async_copy(v_hbm.at[page, :, h, :], vbuf.at[sl, p, h],
                                                 sem.at[sl]))
        return out

    @pl.when(g == 0)
    def _():
        for ahead in range(RING_SLOTS - 1):
            for c in page_copies(g + ahead):
                c.start()

    @pl.when(g + RING_SLOTS - 1 < n_total)
    def _():
        for c in page_copies(g + RING_SLOTS - 1):
            c.start()

    @pl.when(j == 0)
    def _():
        m_ref[...] = jnp.full(m_ref.shape, NEG, F32)
        l_ref[...] = jnp.zeros(l_ref.shape, F32)
        acc_ref[...] = jnp.zeros(acc_ref.shape, F32)

    for c in page_copies(g):
        c.wait()

    def process(k_tiles, v_tiles, bias):
        n_p = len(k_tiles)
        kk = jnp.concatenate(
            [jnp.concatenate([t().astype(BF16) for t in k_tiles[p]], axis=1) for p in range(n_p)],
            axis=0)
        s = jnp.dot(kk, qbd_ref[0], preferred_element_type=F32) + bias
        m_prev = m_ref[...]
        m_new = jnp.maximum(m_prev, jnp.max(s, axis=0, keepdims=True))
        p_t = jnp.exp2(s - m_new)
        alpha = jnp.exp2(m_prev - m_new)
        m_ref[...] = m_new
        l_ref[...] = alpha * l_ref[...] + jnp.sum(p_t, axis=0, keepdims=True)
        pb = p_t.T.astype(BF16)
        heads = []
        for h in range(n_heads):
            vh = jnp.concatenate([v_tiles[p][h]().astype(BF16) for p in range(n_p)], axis=0)
            heads.append(jnp.dot(pb[h * rows:(h + 1) * rows], vh, preferred_element_type=F32))
        alpha_col = jnp.broadcast_to(alpha, (8, cols)).T[:, :1]
        acc_ref[...] = alpha_col * acc_ref[...] + jnp.concatenate(heads, axis=0)

    k_tiles = [[(lambda p=p, h=h: kbuf[slot, p, h]) for h in range(n_heads)] for p in range(n_pg)]
    v_tiles = [[(lambda p=p, h=h: vbuf[slot, p, h]) for h in range(n_heads)] for p in range(n_pg)]
    process(k_tiles, v_tiles, bias_ref[0])

    @pl.when(j == n_steps - 1)
    def _():
        kn = [[(lambda h=h: kn_ref[0, :, h * LANES:(h + 1) * LANES]) for h in range(n_heads)]]
        vn = [[(lambda h=h: vn_ref[0, :, h * LANES:(h + 1) * LANES]) for h in range(n_heads)]]
        process(kn, vn, bnew_ref[...])
        l_col = jnp.broadcast_to(l_ref[...], (8, cols)).T[:, :1]
        o = acc_ref[...] / l_col
        gain = sg_ref[...] * post_scale
        for h in range(n_heads):
            d = o[h * rows:h * rows + t_len] - lam_ref[0] * o[h * rows + t_len:(h + 1) * rows]
            y = d * lax.rsqrt(jnp.mean(d * d, axis=-1, keepdims=True) + EPS) * gain
            o_ref[0, :, h * LANES:(h + 1) * LANES] = y


def _attn_a_sample(qbd, cache_k, cache_v, k_new, v_new, page_table, page_off, lam,
                   bias_steps, bias_new, subln, post_scale, n_heads, t_len):
    bsz, n_pages = page_table.shape
    n_pg = PAGES_PER_STEP
    n_steps = n_pages // n_pg
    d, cols = qbd.shape[1:]
    hd = d // n_heads
    grid_spec = pltpu.PrefetchScalarGridSpec(
        num_scalar_prefetch=1,
        grid=(bsz, n_steps),
        in_specs=[pl.BlockSpec(memory_space=pltpu.SMEM),
                  pl.BlockSpec((1, d, cols), lambda b, j, pt: (b, 0, 0)),
                  pl.BlockSpec(memory_space=pl.ANY),
                  pl.BlockSpec(memory_space=pl.ANY),
                  pl.BlockSpec((1, PAGE_SIZE, d), lambda b, j, pt: (b, 0, 0)),
                  pl.BlockSpec((1, PAGE_SIZE, d), lambda b, j, pt: (b, 0, 0)),
                  pl.BlockSpec((1, n_pg * PAGE_SIZE, cols),
                               lambda b, j, pt: (jnp.where(j == n_steps - 1, 1, 0), 0, 0)),
                  pl.BlockSpec((PAGE_SIZE, cols), lambda b, j, pt: (0, 0)),
                  pl.BlockSpec((1, LANES), lambda b, j, pt: (0, 0))],
        out_specs=pl.BlockSpec((1, t_len, d), lambda b, j, pt: (b, 0, 0)),
        scratch_shapes=[pltpu.VMEM((RING_SLOTS, n_pg, n_heads, PAGE_SIZE, hd), F32),
                        pltpu.VMEM((RING_SLOTS, n_pg, n_heads, PAGE_SIZE, hd), F32),
                        pltpu.SemaphoreType.DMA((RING_SLOTS,)),
                        pltpu.VMEM((1, cols), F32), pltpu.VMEM((1, cols), F32),
                        pltpu.VMEM((cols, hd), F32)],
    )
    return pl.pallas_call(
        functools.partial(_attn_a_sample_kernel, n_pg=n_pg, n_heads=n_heads, t_len=t_len,
                          page_off=page_off, post_scale=post_scale),
        grid_spec=grid_spec,
        out_shape=jax.ShapeDtypeStruct((bsz, t_len, d), F32),
        compiler_params=_params(("arbitrary", "arbitrary")),
        name="attn_a_sample",
    )(page_table, lam, qbd, cache_k, cache_v, k_new, v_new, bias_steps, bias_new,
      subln.reshape(1, LANES))


def _sink_softmax(s, sink_col):
    m = jnp.maximum(jnp.max(s, axis=-1, keepdims=True), sink_col)
    e = jnp.exp(s - m)
    denom = jnp.sum(e, axis=-1, keepdims=True) + jnp.exp(sink_col - m)
    return (e / denom).astype(BF16)


def _swa_prompt_kernel(sink_ref, q_ref, kp_ref, kc_ref, vtp_ref, vtc_ref, bias_ref, o_ref,
                       *, kv_heads, group):
    n = pl.program_id(1)
    w = WINDOW
    key = lax.broadcasted_iota(jnp.int32, (2 * w, group * w), 0)
    qry = lax.broadcasted_iota(jnp.int32, (2 * w, group * w), 1) & (w - 1)
    dist = w + qry - key
    first_key = jnp.where(n > 0, 0, w)
    valid = (dist >= 0) & (dist < w) & (key >= first_key)
    for kv in range(kv_heads):
        hs = [kv * group + g for g in range(group)]
        q = jnp.concatenate([q_ref[0, :, h * LANES:(h + 1) * LANES] for h in hs], axis=0)
        ksl = slice(kv * LANES, (kv + 1) * LANES)
        kcat = jnp.concatenate([kp_ref[0, :, ksl], kc_ref[0, :, ksl]], axis=0)
        vtcat = jnp.concatenate([vtp_ref[0, ksl, :], vtc_ref[0, ksl, :]], axis=1)
        s = lax.dot_general(kcat, q, _NT, preferred_element_type=F32)
        s = s + jnp.concatenate([bias_ref[h] for h in hs], axis=1)
        s = jnp.where(valid, s, NEG)
        sink = jnp.concatenate([jnp.full((1, w), sink_ref[h], F32) for h in hs], axis=1)
        m = jnp.maximum(jnp.max(s, axis=0, keepdims=True), sink)
        e = jnp.exp(s - m)
        denom = jnp.sum(e, axis=0, keepdims=True) + jnp.exp(sink - m)
        o_t = jnp.dot(vtcat, (e / denom).astype(BF16), preferred_element_type=F32)
        for g, h in enumerate(hs):
            o_ref[0, :, h * LANES:(h + 1) * LANES] = o_t[:, g * w:(g + 1) * w].T.astype(o_ref.dtype)


def _swa_prompt(q, k, vt, sinks, bias, kv_heads, group):
    bsz, s_len, d = q.shape
    nb = s_len // WINDOW
    kw = kv_heads * LANES
    prev = lambda b, n: (b, jnp.maximum(n - 1, 0), 0)
    cur = lambda b, n: (b, n, 0)
    prev_t = lambda b, n: (b, 0, jnp.maximum(n - 1, 0))
    cur_t = lambda b, n: (b, 0, n)
    return pl.pallas_call(
        functools.partial(_swa_prompt_kernel, kv_heads=kv_heads, group=group),
        grid=(bsz, nb),
        in_specs=[pl.BlockSpec(memory_space=pltpu.SMEM),
                  pl.BlockSpec((1, WINDOW, d), cur),
                  pl.BlockSpec((1, WINDOW, kw), prev),
                  pl.BlockSpec((1, WINDOW, kw), cur),
                  pl.BlockSpec((1, kw, WINDOW), prev_t),
                  pl.BlockSpec((1, kw, WINDOW), cur_t),
                  pl.BlockSpec(bias.shape, lambda b, n: (0, 0, 0))],
        out_specs=pl.BlockSpec((1, WINDOW, d), cur),
        out_shape=jax.ShapeDtypeStruct(q.shape, BF16),
        compiler_params=_params(("parallel", "arbitrary")),
        name="swa_prompt",
    )(sinks, q, k, k, vt, vt, bias)


def _swa_sample_kernel(sink_ref, q_ref, k_ref, v_ref, bias_ref, o_ref, *, kv_heads, group, t_len):
    w = WINDOW
    for kv in range(kv_heads):
        q = jnp.concatenate(
            [q_ref[0, :, (kv * group + g) * LANES:(kv * group + g + 1) * LANES]
             for g in range(group)], axis=0).astype(BF16)
        kk = k_ref[0, :, kv * LANES:(kv + 1) * LANES]
        vv = v_ref[0, :, kv * LANES:(kv + 1) * LANES]
        s = lax.dot_general(q, kk, _NT, preferred_element_type=F32)
        s = s + bias_ref[kv * group:(kv + 1) * group].reshape(group * t_len, 2 * w)
        row = lax.broadcasted_iota(jnp.int32, s.shape, 0) & (t_len - 1)
        col = lax.broadcasted_iota(jnp.int32, s.shape, 1)
        dist = w + row - col
        s = jnp.where((dist >= 0) & (dist < w), s, NEG)
        sink_col = jnp.concatenate(
            [jnp.full((t_len, 1), sink_ref[kv * group + g], F32) for g in range(group)], axis=0)
        o = jnp.dot(_sink_softmax(s, sink_col), vv, preferred_element_type=F32)
        for g in range(group):
            hh = kv * group + g
            o_ref[0, :, hh * LANES:(hh + 1) * LANES] = o[g * t_len:(g + 1) * t_len]


def _swa_sample(q, kk, vv, sinks, bias, kv_heads, group):
    bsz, t_len, d = q.shape
    n_keys = kk.shape[1]
    return pl.pallas_call(
        functools.partial(_swa_sample_kernel, kv_heads=kv_heads, group=group, t_len=t_len),
        grid=(bsz,),
        in_specs=[pl.BlockSpec(memory_space=pltpu.SMEM),
                  pl.BlockSpec((1, t_len, d), lambda b: (b, 0, 0)),
                  pl.BlockSpec((1, n_keys, kv_heads * LANES), lambda b: (b, 0, 0)),
                  pl.BlockSpec((1, n_keys, kv_heads * LANES), lambda b: (b, 0, 0)),
                  pl.BlockSpec(bias.shape, lambda b: (0, 0, 0))],
        out_specs=pl.BlockSpec((1, t_len, d), lambda b: (b, 0, 0)),
        out_shape=jax.ShapeDtypeStruct(q.shape, F32),
        compiler_params=_params(("parallel",)),
        name="swa_sample",
    )(sinks, q, kk, vv, bias)


def _sample_bias_tiles(rel_t, n_heads, t_len):
    cols = n_heads * 2 * t_len

    def tile(per_key):
        n_keys = per_key.shape[-1]
        t = jnp.transpose(per_key, (2, 0, 1))[:, :, None, :]
        return jnp.broadcast_to(t, (n_keys, n_heads, 2, t_len)).reshape(n_keys, cols)

    last = tile(rel_t[:, :, :PAGE_SIZE])
    zero = jnp.zeros(((PAGES_PER_STEP - 1) * PAGE_SIZE, cols), F32)
    steps = jnp.stack([jnp.zeros((PAGES_PER_STEP * PAGE_SIZE, cols), F32),
                       jnp.concatenate([zero, last], axis=0)])
    new = tile(rel_t[:, :, PAGE_SIZE:PAGE_SIZE + t_len])
    new = jnp.concatenate([new, jnp.full((PAGE_SIZE - t_len, cols), NEG, F32)], axis=0)
    return steps, new


def _block_diag_queries(q, n_heads, dqk):
    bsz, t_len, _ = q.shape
    qt = jnp.transpose(q.reshape(bsz, t_len, n_heads, 2, dqk), (0, 2, 3, 4, 1))
    comp = jnp.eye(2, dtype=q.dtype)
    blk = jnp.einsum("bhcdt,ce->bhcdet", qt, comp).reshape(bsz, n_heads, 2 * dqk, 2 * t_len)
    head = jnp.eye(n_heads, dtype=q.dtype)
    full = jnp.einsum("bhdk,hg->bhdgk", blk, head)
    return full.reshape(bsz, n_heads * 2 * dqk, n_heads * 2 * t_len)


def _trunk(x, sample, caches, bias, weights, dims):
    (cache_a_k, cache_a_v, cache_b_k, cache_b_v, page_table) = caches
    (norm_ffn1, norm_attn, norm_ffn2, w_ffn_in, w_ffn_out, w_a_qkv, a_q_norm, a_k_norm,
     a_lambda, a_subln, w_a_o, norm_kv, w_b_kv, b_k_norm, w_b_q, b_q_norm, b_sinks, w_b_o) = weights
    depth, n_a, n_heads, kv_heads = dims
    bsz, t_len, d = x.shape
    m = bsz * t_len
    group = n_heads // kv_heads
    dqk = d // n_heads // 2
    hd = d // n_heads
    kv_w = kv_heads * hd
    tm = min(512, m)
    d_ff = w_ffn_out.shape[2]
    tf_cap = 1536 if m <= 256 else 512
    tf = max(t for t in range(LANES, tf_cap + 1, LANES) if d_ff % t == 0)
    x = x.reshape(m, d)
    a_ks, a_vs = [], []
    kv_k = kv_v = kb16 = vb16 = None
    for li in range(depth):
        if li == n_a:
            kv_k, kb16 = _proj(x, w_b_kv, n=kv_w, g=norm_kv, head_gain=b_k_norm, seg=hd,
                               out_dtypes=(F32, BF16), tm=tm)
            kv_v, vb16 = _proj(x, w_b_kv, n=kv_w, col_off=kv_w, g=norm_kv,
                               out_dtypes=(F32, BF16), t_rows=0 if sample else t_len, tm=tm)
        x = _ffn(x, norm_ffn1[li], w_ffn_in, w_ffn_out, li, 0, tm=tm, tf=tf)
        q_dt = F32 if sample else BF16
        if li < n_a:
            lam_init = 0.8 - 0.6 * math.exp(-0.3 * li)
            lam = _diff_lambda(a_lambda[li], lam_init)
            (q,) = _proj(x, w_a_qkv[li], n=d, g=norm_attn[li], head_gain=a_q_norm[li], seg=dqk,
                         scale=dqk ** -0.5 * LOG2E, out_dtypes=(q_dt,), tm=tm)
            k, k16 = _proj(x, w_a_qkv[li], n=d, col_off=d, g=norm_attn[li],
                           head_gain=a_k_norm[li], seg=dqk, out_dtypes=(F32, BF16), tm=tm)
            v, v16 = _proj(x, w_a_qkv[li], n=d, col_off=2 * d, g=norm_attn[li],
                           out_dtypes=(F32, BF16), t_rows=0 if sample else t_len, tm=tm)
            a_ks.append(k.reshape(bsz, t_len, n_heads, hd))
            a_vs.append(v.reshape(bsz, t_len, n_heads, hd))
            if sample:
                n_pool = cache_a_k.shape[1]
                qbd = _block_diag_queries(q.reshape(bsz, t_len, d), n_heads, dqk).astype(BF16)
                pad = ((0, 0), (0, PAGE_SIZE - t_len), (0, 0))
                o = _attn_a_sample(
                    qbd,
                    cache_a_k.reshape((-1,) + cache_a_k.shape[2:]),
                    cache_a_v.reshape((-1,) + cache_a_v.shape[2:]),
                    jnp.pad(k.reshape(bsz, t_len, d), pad), jnp.pad(v.reshape(bsz, t_len, d), pad),
                    page_table, li * n_pool, lam, bias["a_steps"], bias["a_new"],
                    a_subln[li], 1.0 - lam_init, n_heads, t_len)
            else:
                o = _attn_a_prompt(q.reshape(bsz, t_len, d), k16.reshape(bsz, t_len, d), v16, lam,
                                   bias["a_tail"], a_subln[li], 1.0 - lam_init, n_heads)
            (x,) = _proj(o.reshape(m, d), w_a_o[li], n=d, res=x, tm=tm)
        else:
            bi = li - n_a
            (q,) = _proj(x, w_b_q[bi], n=d, g=norm_attn[li], head_gain=b_q_norm[bi], seg=hd,
                         scale=hd ** -0.5, out_dtypes=(q_dt,), tm=tm)
            if sample:
                zpad = jnp.zeros((bsz, WINDOW - t_len, kv_w), BF16)
                kk = jnp.concatenate([cache_b_k.reshape(bsz, WINDOW, kv_w).astype(BF16),
                                      kb16.reshape(bsz, t_len, kv_w), zpad], axis=1)
                vv = jnp.concatenate([cache_b_v.reshape(bsz, WINDOW, kv_w).astype(BF16),
                                      vb16.reshape(bsz, t_len, kv_w), zpad], axis=1)
                o = _swa_sample(q.reshape(bsz, t_len, d), kk, vv, b_sinks[bi], bias["swa_s"],
                                kv_heads, group)
            else:
                o = _swa_prompt(q.reshape(bsz, t_len, d), kb16.reshape(bsz, t_len, kv_w), vb16,
                                b_sinks[bi], bias["swa_p"], kv_heads, group)
            (x,) = _proj(o.reshape(m, d), w_b_o[bi], n=d, res=x, tm=tm)
        x = _ffn(x, norm_ffn2[li], w_ffn_in, w_ffn_out, li, 1, tm=tm, tf=tf)
    kv_k = kv_k.reshape(bsz, t_len, kv_heads, hd)
    kv_v = kv_v.reshape(bsz, t_len, kv_heads, hd)
    if sample:
        kv_k = jnp.concatenate([cache_b_k, kv_k], axis=1)
        kv_v = jnp.concatenate([cache_b_v, kv_v], axis=1)
    keep = min(WINDOW, kv_k.shape[1])
    return (x.reshape(bsz, t_len, d), jnp.stack(a_ks), jnp.stack(a_vs),
            kv_k[:, -keep:], kv_v[:, -keep:])


def kernel(x_prompt, x_sample, cache_a_k, cache_a_v, cache_b_k, cache_b_v, page_table, rel_bias_table, norm_ffn1, norm_attn, norm_ffn2, w_ffn_in, w_ffn_out, w_a_qkv, a_q_norm, a_k_norm, a_lambda, a_subln, w_a_o, norm_kv, w_b_kv, b_k_norm, w_b_q, b_q_norm, b_sinks, w_b_o):
    depth = norm_ffn1.shape[0]
    n_a = w_a_qkv.shape[0]
    n_heads = rel_bias_table.shape[1]
    kv_heads = cache_b_k.shape[2]
    t_len = x_sample.shape[1]
    dims = (depth, n_a, n_heads, kv_heads)

    tq = TQ_A
    kc = np.arange(2 * tq)[:, None]
    qr = np.arange(tq)[None, :]
    tail0 = _bias_tiles(rel_bias_table, _bucket_np(qr - kc), rel_far=True, scale=LOG2E)
    tail1 = _bias_tiles(rel_bias_table, _bucket_np(tq + qr - kc), rel_far=True, scale=LOG2E)
    rw = np.arange(WINDOW)[:, None]
    cw = np.arange(2 * WINDOW)[None, :]
    swa_idx = _bucket_np(WINDOW + rw - cw)
    swa_p = _bias_tiles(rel_bias_table, np.maximum(swa_idx.T, 0))
    swa_s = _bias_tiles(rel_bias_table, np.maximum(swa_idx[:t_len], 0))
    rel_t = _bias_tiles(rel_bias_table, swa_idx[:t_len], rel_far=True, scale=LOG2E)
    a_steps, a_new = _sample_bias_tiles(rel_t, n_heads, t_len)
    bias = {"a_tail": jnp.stack([tail0, tail1], axis=1), "swa_p": swa_p, "swa_s": swa_s,
            "a_steps": a_steps, "a_new": a_new}

    wcast = lambda w: w.astype(BF16)
    weights = (norm_ffn1, norm_attn, norm_ffn2, wcast(w_ffn_in), wcast(w_ffn_out), wcast(w_a_qkv),
               a_q_norm, a_k_norm, a_lambda, a_subln, wcast(w_a_o), norm_kv, wcast(w_b_kv),
               b_k_norm, wcast(w_b_q), b_q_norm, b_sinks, wcast(w_b_o))
    none5 = (None,) * 5
    y_p, ak_p, av_p, bk_p, bv_p = _trunk(x_prompt, False, none5, bias, weights, dims)
    y_s, ak_s, av_s, bk_s, bv_s = _trunk(
        x_sample, True, (cache_a_k, cache_a_v, cache_b_k, cache_b_v, page_table), bias, weights, dims)
    return (y_p, y_s, ak_p, av_p, ak_s, av_s, bk_p, bv_p, bk_s, bv_s)
```

```python
import functools
import math

import numpy as np
import jax
import jax.numpy as jnp
from jax import lax
from jax.experimental import pallas as pl
from jax.experimental.pallas import tpu as pltpu

F32 = jnp.float32
BF16 = jnp.bfloat16

EPS = 1e-6
NEG = -1e30
LOG2E = math.log2(math.e)
N_BUCKETS = 32
MAX_DISTANCE = 128
PAGE_SIZE = 128
WINDOW = 128
LANES = 128
VMEM_LIMIT = 56 * 1024 * 1024

TQ_A = 256
HEADS_A = 8
PAGES_PER_STEP = 4
RING_SLOTS = 3

_NT = (((1,), (1,)), ((), ()))


def _params(sem):
    return pltpu.CompilerParams(dimension_semantics=sem, vmem_limit_bytes=VMEM_LIMIT)


def _bucket_np(dist):
    n = np.maximum(dist, 0)
    max_exact = N_BUCKETS // 2
    nf = np.maximum(n, 1).astype(np.float32)
    large = max_exact + (np.log(nf / np.float32(max_exact))
                         / np.float32(math.log(MAX_DISTANCE / max_exact))
                         * np.float32(N_BUCKETS - max_exact)).astype(np.int32)
    bucket = np.where(n < max_exact, n, np.minimum(large, N_BUCKETS - 1))
    return np.where(dist < 0, -1, bucket).astype(np.int32)


def _bias_kernel(tbl_ref, idx_ref, o_ref, *, rel_far, scale):
    h = pl.program_id(0)
    idx = idx_ref[...]
    acc = jnp.zeros(idx.shape, F32)
    for b in range(N_BUCKETS):
        acc = jnp.where(idx == b, tbl_ref[b, h], acc)
    if rel_far:
        acc = jnp.where(idx < 0, NEG, (acc - tbl_ref[N_BUCKETS - 1, h]) * scale)
    o_ref[0] = acc


def _bias_tiles(table, idx_np, *, rel_far=False, scale=1.0):
    n_heads = table.shape[1]
    r, c = idx_np.shape
    return pl.pallas_call(
        functools.partial(_bias_kernel, rel_far=rel_far, scale=scale),
        grid=(n_heads,),
        in_specs=[pl.BlockSpec(memory_space=pltpu.SMEM),
                  pl.BlockSpec((r, c), lambda h: (0, 0))],
        out_specs=pl.BlockSpec((1, r, c), lambda h: (h, 0, 0)),
        out_shape=jax.ShapeDtypeStruct((n_heads, r, c), F32),
        compiler_params=_params(("arbitrary",)),
        name="bias_tiles",
    )(table, jnp.asarray(idx_np))


def _lam_kernel(al_ref, o_ref, *, lam_init):
    a = al_ref[...]
    s1 = jnp.sum(a[0:1] * a[1:2], axis=-1, keepdims=True)
    s2 = jnp.sum(a[2:3] * a[3:4], axis=-1, keepdims=True)
    lam = jnp.exp(s1) - jnp.exp(s2) + lam_init
    o_ref[...] = jnp.broadcast_to(lam, o_ref.shape)


def _diff_lambda(lam_params, lam_init):
    out = pl.pallas_call(
        functools.partial(_lam_kernel, lam_init=lam_init),
        out_shape=jax.ShapeDtypeStruct((8, LANES), F32),
        name="diff_lambda",
    )(lam_params)
    return out[0, :1]


def _ffn_kernel(x_ref, g_ref, wa_ref, wb_ref, wo_ref, o_ref, xn_ref, hid_ref, acc_ref):
    f = pl.program_id(1)
    nf = pl.num_programs(1) - 1

    def up():
        xn = xn_ref[...]
        a = jnp.dot(xn, wa_ref[...], preferred_element_type=F32)
        b = jnp.dot(xn, wb_ref[...], preferred_element_type=F32)
        return (a * (1.0 / (1.0 + jnp.exp(-a))) * b).astype(BF16)

    @pl.when(f == 0)
    def _():
        xf = x_ref[...]
        y = xf * lax.rsqrt(jnp.mean(xf * xf, axis=-1, keepdims=True) + EPS) * g_ref[...]
        xn_ref[...] = y.astype(BF16)
        acc_ref[...] = jnp.zeros_like(acc_ref)
        hid_ref[...] = up()

    @pl.when(jnp.logical_and(f > 0, f < nf))
    def _():
        acc_ref[...] += jnp.dot(hid_ref[...], wo_ref[...], preferred_element_type=F32)
        hid_ref[...] = up()

    @pl.when(f == nf)
    def _():
        acc = acc_ref[...] + jnp.dot(hid_ref[...], wo_ref[...], preferred_element_type=F32)
        o_ref[...] = x_ref[...] + 0.5 * acc


def _ffn(x, g, w_in, w_out, li, k, *, tm, tf):
    m, d = x.shape
    d_ff = w_out.shape[2]
    nf = d_ff // tf
    return pl.pallas_call(
        _ffn_kernel,
        grid=(m // tm, nf + 1),
        in_specs=[pl.BlockSpec((tm, d), lambda i, f: (i, 0)),
                  pl.BlockSpec((1, d), lambda i, f: (0, 0)),
                  pl.BlockSpec((None, None, d, tf), lambda i, f: (li, k, 0, jnp.minimum(f, nf - 1))),
                  pl.BlockSpec((None, None, d, tf),
                               lambda i, f: (li, k, 0, jnp.minimum(f, nf - 1) + nf)),
                  pl.BlockSpec((None, None, tf, d), lambda i, f: (li, k, jnp.maximum(f - 1, 0), 0))],
        out_specs=pl.BlockSpec((tm, d), lambda i, f: (i, 0)),
        out_shape=jax.ShapeDtypeStruct((m, d), F32),
        scratch_shapes=[pltpu.VMEM((tm, d), BF16), pltpu.VMEM((tm, tf), BF16),
                        pltpu.VMEM((tm, d), F32)],
        compiler_params=_params(("parallel", "arbitrary")),
        name="ffn",
    )(x, g.reshape(1, d), w_in, w_in, w_out)


def _proj_kernel(*refs, rms, seg, scale, residual, n_out, t_out):
    refs = list(refs)
    x_ref = refs.pop(0)
    g_ref = refs.pop(0) if rms else None
    w_ref = refs.pop(0)
    res_ref = refs.pop(0) if residual else None
    hg_ref = refs.pop(0) if seg else None
    o_refs = [refs.pop(0) for _ in range(n_out)]
    xn_ref = refs.pop(0)

    @pl.when(pl.program_id(1) == 0)
    def _():
        xf = x_ref[...].astype(F32)
        if rms:
            xf = xf * lax.rsqrt(jnp.mean(xf * xf, axis=-1, keepdims=True) + EPS) * g_ref[...]
        xn_ref[...] = xf.astype(BF16)

    acc = jnp.dot(xn_ref[...], w_ref[...], preferred_element_type=F32)
    tm, tn = acc.shape
    if residual:
        acc = res_ref[...] + acc
    if not seg:
        if t_out:
            t_ref = o_refs.pop()
            t_ref[0] = acc.T.astype(t_ref.dtype)
        for o_ref in o_refs:
            o_ref[...] = acc.astype(o_ref.dtype)
        return
    lane = lax.broadcasted_iota(jnp.int32, (tm, LANES), 1)
    low = lane < seg
    gain = hg_ref[...] * scale
    for cb in range(tn // LANES):
        blk = acc[:, cb * LANES:(cb + 1) * LANES]
        sq = blk * blk
        if seg == LANES:
            ms = jnp.mean(sq, axis=-1, keepdims=True)
        else:
            s_lo = jnp.sum(jnp.where(low, sq, 0.0), axis=-1, keepdims=True)
            s_hi = jnp.sum(jnp.where(low, 0.0, sq), axis=-1, keepdims=True)
            ms = jnp.where(low, s_lo, s_hi) * (1.0 / seg)
        y = blk * lax.rsqrt(ms + EPS) * gain
        for o_ref in o_refs:
            o_ref[:, cb * LANES:(cb + 1) * LANES] = y.astype(o_ref.dtype)


def _proj(x, w, *, n, col_off=0, g=None, head_gain=None, seg=0, scale=1.0, res=None,
          out_dtypes=(F32,), t_rows=0, tm=512, tn=2048):
    m, d = x.shape
    tm = min(tm, m)
    tn = min(tn, n)
    off = col_off // tn
    out_specs = [pl.BlockSpec((tm, tn), lambda i, j: (i, j)) for _ in out_dtypes]
    out_shape = [jax.ShapeDtypeStruct((m, n), dt) for dt in out_dtypes]
    if t_rows:
        assert not seg and t_rows % tm == 0
        per = t_rows // tm
        out_specs[-1] = pl.BlockSpec((1, tn, tm), lambda i, j: (i // per, j, i % per))
        out_shape[-1] = jax.ShapeDtypeStruct((m // t_rows, n, t_rows), out_dtypes[-1])
    ins = [x]
    specs = [pl.BlockSpec((tm, d), lambda i, j: (i, 0))]
    if g is not None:
        ins.append(g.reshape(1, d))
        specs.append(pl.BlockSpec((1, d), lambda i, j: (0, 0)))
    ins.append(w)
    specs.append(pl.BlockSpec((d, tn), lambda i, j: (0, j + off)))
    if res is not None:
        ins.append(res)
        specs.append(pl.BlockSpec((tm, tn), lambda i, j: (i, j)))
    if seg:
        hg = jnp.tile(head_gain.astype(F32), LANES // seg).reshape(1, LANES)
        ins.append(hg)
        specs.append(pl.BlockSpec((1, LANES), lambda i, j: (0, 0)))
    outs = pl.pallas_call(
        functools.partial(_proj_kernel, rms=g is not None, seg=seg, scale=scale,
                          residual=res is not None, n_out=len(out_dtypes), t_out=bool(t_rows)),
        grid=(m // tm, n // tn),
        in_specs=specs,
        out_specs=out_specs,
        out_shape=out_shape,
        scratch_shapes=[pltpu.VMEM((tm, d), BF16)],
        compiler_params=_params(("parallel", "arbitrary")),
        name="proj",
    )(*ins)
    return outs


def _attn_a_prompt_kernel(lam_ref, q_ref, k_ref, vt_ref, bt_ref, sg_ref, o_ref,
                          m_ref, l_ref, acc_ref, *, tq, dqk, n_hd, post_scale):
    i = pl.program_id(2)
    lane = lax.broadcasted_iota(jnp.int32, (tq, LANES), 1)
    qqs = []
    for g in range(n_hd):
        q = q_ref[0, :, g * LANES:(g + 1) * LANES]
        zero = jnp.zeros_like(q)
        qqs.append(jnp.concatenate([jnp.where(lane < dqk, q, zero),
                                    jnp.where(lane < dqk, zero, q)], axis=0))
    m_ref[...] = jnp.full(m_ref.shape, NEG, F32)
    l_ref[...] = jnp.zeros(l_ref.shape, F32)
    acc_ref[...] = jnp.zeros(acc_ref.shape, F32)

    def update(start, size, biases):
        for g in range(n_hd):
            kblk = k_ref[0, pl.ds(start, size), g * LANES:(g + 1) * LANES]
            vtblk = vt_ref[0, g * LANES:(g + 1) * LANES, pl.ds(start, size)]
            s = lax.dot_general(kblk, qqs[g], _NT, preferred_element_type=F32)
            if biases is not None:
                s = s + biases[g]
            m_prev = m_ref[g]
            m_new = jnp.maximum(m_prev, jnp.max(s, axis=0, keepdims=True))
            p = jnp.exp2(s - m_new)
            alpha = jnp.exp2(m_prev - m_new)
            m_ref[g] = m_new
            l_ref[g] = alpha * l_ref[g] + jnp.sum(p, axis=0, keepdims=True)
            acc_ref[g] = alpha * acc_ref[g] + jnp.dot(vtblk, p.astype(BF16),
                                                      preferred_element_type=F32)

    tails = []
    for g in range(n_hd):
        b = bt_ref[g, 0]
        tails.append(jnp.concatenate([b, b], axis=1))
    update(pl.multiple_of(jnp.maximum(i - 1, 0) * tq, tq), 2 * tq, tails)

    def far(f, c):
        update(pl.multiple_of(f * 2 * tq, 2 * tq), 2 * tq, None)
        return c
    lax.fori_loop(0, jnp.maximum(i - 1, 0) // 2, far, 0)

    @pl.when(jnp.logical_and(i >= 2, (i & 1) == 0))
    def _():
        update(pl.multiple_of((i - 2) * tq, tq), tq, None)

    gain = sg_ref[...] * post_scale
    for g in range(n_hd):
        o = acc_ref[g] / l_ref[g]
        d = o[:, :tq] - lam_ref[0] * o[:, tq:]
        y = d * lax.rsqrt(jnp.mean(d * d, axis=0, keepdims=True) + EPS) * gain
        o_ref[0, :, g * LANES:(g + 1) * LANES] = y.T.astype(o_ref.dtype)


def _attn_a_prompt(q, k, vt, lam, bias_tail, subln, post_scale, n_heads):
    bsz, s_len, _ = q.shape
    tq, n_hd = TQ_A, HEADS_A
    gw = n_hd * LANES
    return pl.pallas_call(
        functools.partial(_attn_a_prompt_kernel, tq=tq, dqk=LANES // 2, n_hd=n_hd,
                          post_scale=post_scale),
        grid=(bsz, n_heads // n_hd, s_len // tq),
        in_specs=[pl.BlockSpec(memory_space=pltpu.SMEM),
                  pl.BlockSpec((1, tq, gw), lambda b, h, i: (b, i, h)),
                  pl.BlockSpec((1, s_len, gw), lambda b, h, i: (b, 0, h)),
                  pl.BlockSpec((1, gw, s_len), lambda b, h, i: (b, h, 0)),
                  pl.BlockSpec((n_hd, 1, 2 * tq, tq), lambda b, h, i: (h, jnp.minimum(i, 1), 0, 0)),
                  pl.BlockSpec((LANES, 1), lambda b, h, i: (0, 0))],
        out_specs=pl.BlockSpec((1, tq, gw), lambda b, h, i: (b, i, h)),
        out_shape=jax.ShapeDtypeStruct(q.shape, BF16),
        scratch_shapes=[pltpu.VMEM((n_hd, 1, 2 * tq), F32), pltpu.VMEM((n_hd, 1, 2 * tq), F32),
                        pltpu.VMEM((n_hd, LANES, 2 * tq), F32)],
        compiler_params=_params(("parallel", "parallel", "arbitrary")),
        name="attn_a_prompt",
    )(lam, q, k, vt, bias_tail, subln.reshape(LANES, 1))


def _attn_a_sample_kernel(pt_ref, lam_ref, qbd_ref, k_hbm, v_hbm, kn_ref, vn_ref, bias_ref,
                          bnew_ref, sg_ref, o_ref, kbuf, vbuf, sem, m_ref, l_ref, acc_ref,
                          *, n_pg, n_heads, t_len, page_off, post_scale):
    b = pl.program_id(0)
    j = pl.program_id(1)
    n_steps = pl.num_programs(1)
    n_total = pl.num_programs(0) * n_steps
    g = b * n_steps + j
    slot = g % RING_SLOTS
    rows = 2 * t_len
    cols = n_heads * rows

    def page_copies(step):
        bb = step // n_steps
        jj = step % n_steps
        sl = step % RING_SLOTS
        out = []
        for p in range(n_pg):
            page = pt_ref[bb, jj * n_pg + p] + page_off
            for h in range(n_heads):
                out.append(pltpu.make_async_copy(k_hbm.at[page, :, h, :], kbuf.at[sl, p, h],
                                                 sem.at[sl]))
                out.append(pltpu.make_async_copy(v_hbm.at[page, :, h, :], vbuf.at[sl, p, h],
                                                 sem.at[sl]))
        return out

    @pl.when(g == 0)
    def _():
        for ahead in range(RING_SLOTS - 1):
            for c in page_copies(g + ahead):
                c.start()

    @pl.when(g + RING_SLOTS - 1 < n_total)
    def _():
        for c in page_copies(g + RING_SLOTS - 1):
            c.start()

    @pl.when(j == 0)
    def _():
        m_ref[...] = jnp.full(m_ref.shape, NEG, F32)
        l_ref[...] = jnp.zeros(l_ref.shape, F32)
        acc_ref[...] = jnp.zeros(acc_ref.shape, F32)

    for c in page_copies(g):
        c.wait()

    def process(k_tiles, v_tiles, bias):
        n_p = len(k_tiles)
        kk = jnp.concatenate(
            [jnp.concatenate([t().astype(BF16) for t in k_tiles[p]], axis=1) for p in range(n_p)],
            axis=0)
        s = jnp.dot(kk, qbd_ref[0], preferred_element_type=F32) + bias
        m_prev = m_ref[...]
        m_new = jnp.maximum(m_prev, jnp.max(s, axis=0, keepdims=True))
        p_t = jnp.exp2(s - m_new)
        alpha = jnp.exp2(m_prev - m_new)
        m_ref[...] = m_new
        l_ref[...] = alpha * l_ref[...] + jnp.sum(p_t, axis=0, keepdims=True)
        pb = p_t.T.astype(BF16)
        heads = []
        for h in range(n_heads):
            vh = jnp.concatenate([v_tiles[p][h]().astype(BF16) for p in range(n_p)], axis=0)
            heads.append(jnp.dot(pb[h * rows:(h + 1) * rows], vh, preferred_element_type=F32))
        alpha_col = jnp.broadcast_to(alpha, (8, cols)).T[:, :1]
        acc_ref[...] = alpha_col * acc_ref[...] + jnp.concatenate(heads, axis=0)

    k_tiles = [[(lambda p=p, h=h: kbuf[slot, p, h]) for h in range(n_heads)] for p in range(n_pg)]
    v_tiles = [[(lambda p=p, h=h: vbuf[slot, p, h]) for h in range(n_heads)] for p in range(n_pg)]
    process(k_tiles, v_tiles, bias_ref[0])

    @pl.when(j == n_steps - 1)
    def _():
        kn = [[(lambda h=h: kn_ref[0, :, h * LANES:(h + 1) * LANES]) for h in range(n_heads)]]
        vn = [[(lambda h=h: vn_ref[0, :, h * LANES:(h + 1) * LANES]) for h in range(n_heads)]]
        process(kn, vn, bnew_ref[...])
        l_col = jnp.broadcast_to(l_ref[...], (8, cols)).T[:, :1]
        o = acc_ref[...] / l_col
        gain = sg_ref[...] * post_scale
        for h in range(n_heads):
            d = o[h * rows:h * rows + t_len] - lam_ref[0] * o[h * rows + t_len:(h + 1) * rows]
            y = d * lax.rsqrt(jnp.mean(d * d, axis=-1, keepdims=True) + EPS) * gain
            o_ref[0, :, h * LANES:(h + 1) * LANES] = y


def _attn_a_sample(qbd, cache_k, cache_v, k_new, v_new, page_table, page_off, lam,
                   bias_steps, bias_new, subln, post_scale, n_heads, t_len):
    bsz, n_pages = page_table.shape
    n_pg = PAGES_PER_STEP
    n_steps = n_pages // n_pg
    d, cols = qbd.shape[1:]
    hd = d // n_heads
    grid_spec = pltpu.PrefetchScalarGridSpec(
        num_scalar_prefetch=1,
        grid=(bsz, n_steps),
        in_specs=[pl.BlockSpec(memory_space=pltpu.SMEM),
                  pl.BlockSpec((1, d, cols), lambda b, j, pt: (b, 0, 0)),
                  pl.BlockSpec(memory_space=pl.ANY),
                  pl.BlockSpec(memory_space=pl.ANY),
                  pl.BlockSpec((1, PAGE_SIZE, d), lambda b, j, pt: (b, 0, 0)),
                  pl.BlockSpec((1, PAGE_SIZE, d), lambda b, j, pt: (b, 0, 0)),
                  pl.BlockSpec((1, n_pg * PAGE_SIZE, cols),
                               lambda b, j, pt: (jnp.where(j == n_steps - 1, 1, 0), 0, 0)),
                  pl.BlockSpec((PAGE_SIZE, cols), lambda b, j, pt: (0, 0)),
                  pl.BlockSpec((1, LANES), lambda b, j, pt: (0, 0))],
        out_specs=pl.BlockSpec((1, t_len, d), lambda b, j, pt: (b, 0, 0)),
        scratch_shapes=[pltpu.VMEM((RING_SLOTS, n_pg, n_heads, PAGE_SIZE, hd), F32),
                        pltpu.VMEM((RING_SLOTS, n_pg, n_heads, PAGE_SIZE, hd), F32),
                        pltpu.SemaphoreType.DMA((RING_SLOTS,)),
                        pltpu.VMEM((1, cols), F32), pltpu.VMEM((1, cols), F32),
                        pltpu.VMEM((cols, hd), F32)],
    )
    return pl.pallas_call(
        functools.partial(_attn_a_sample_kernel, n_pg=n_pg, n_heads=n_heads, t_len=t_len,
                          page_off=page_off, post_scale=post_scale),
        grid_spec=grid_spec,
        out_shape=jax.ShapeDtypeStruct((bsz, t_len, d), F32),
        compiler_params=_params(("arbitrary", "arbitrary")),
        name="attn_a_sample",
    )(page_table, lam, qbd, cache_k, cache_v, k_new, v_new, bias_steps, bias_new,
      subln.reshape(1, LANES))


def _sink_softmax(s, sink_col):
    m = jnp.maximum(jnp.max(s, axis=-1, keepdims=True), sink_col)
    e = jnp.exp(s - m)
    denom = jnp.sum(e, axis=-1, keepdims=True) + jnp.exp(sink_col - m)
    return (e / denom).astype(BF16)


def _swa_prompt_kernel(sink_ref, q_ref, kp_ref, kc_ref, vtp_ref, vtc_ref, bias_ref, o_ref,
                       *, kv_heads, group):
    n = pl.program_id(1)
    w = WINDOW
    key = lax.broadcasted_iota(jnp.int32, (2 * w, group * w), 0)
    qry = lax.broadcasted_iota(jnp.int32, (2 * w, group * w), 1) & (w - 1)
    dist = w + qry - key
    first_key = jnp.where(n > 0, 0, w)
    valid = (dist >= 0) & (dist < w) & (key >= first_key)
    for kv in range(kv_heads):
        hs = [kv * group + g for g in range(group)]
        q = jnp.concatenate([q_ref[0, :, h * LANES:(h + 1) * LANES] for h in hs], axis=0)
        ksl = slice(kv * LANES, (kv + 1) * LANES)
        kcat = jnp.concatenate([kp_ref[0, :, ksl], kc_ref[0, :, ksl]], axis=0)
        vtcat = jnp.concatenate([vtp_ref[0, ksl, :], vtc_ref[0, ksl, :]], axis=1)
        s = lax.dot_general(kcat, q, _NT, preferred_element_type=F32)
        s = s + jnp.concatenate([bias_ref[h] for h in hs], axis=1)
        s = jnp.where(valid, s, NEG)
        sink = jnp.concatenate([jnp.full((1, w), sink_ref[h], F32) for h in hs], axis=1)
        m = jnp.maximum(jnp.max(s, axis=0, keepdims=True), sink)
        e = jnp.exp(s - m)
        denom = jnp.sum(e, axis=0, keepdims=True) + jnp.exp(sink - m)
        o_t = jnp.dot(vtcat, (e / denom).astype(BF16), preferred_element_type=F32)
        for g, h in enumerate(hs):
            o_ref[0, :, h * LANES:(h + 1) * LANES] = o_t[:, g * w:(g + 1) * w].T.astype(o_ref.dtype)


def _swa_prompt(q, k, vt, sinks, bias, kv_heads, group):
    bsz, s_len, d = q.shape
    nb = s_len // WINDOW
    kw = kv_heads * LANES
    prev = lambda b, n: (b, jnp.maximum(n - 1, 0), 0)
    cur = lambda b, n: (b, n, 0)
    prev_t = lambda b, n: (b, 0, jnp.maximum(n - 1, 0))
    cur_t = lambda b, n: (b, 0, n)
    return pl.pallas_call(
        functools.partial(_swa_prompt_kernel, kv_heads=kv_heads, group=group),
        grid=(bsz, nb),
        in_specs=[pl.BlockSpec(memory_space=pltpu.SMEM),
                  pl.BlockSpec((1, WINDOW, d), cur),
                  pl.BlockSpec((1, WINDOW, kw), prev),
                  pl.BlockSpec((1, WINDOW, kw), cur),
                  pl.BlockSpec((1, kw, WINDOW), prev_t),
                  pl.BlockSpec((1, kw, WINDOW), cur_t),
                  pl.BlockSpec(bias.shape, lambda b, n: (0, 0, 0))],
        out_specs=pl.BlockSpec((1, WINDOW, d), cur),
        out_shape=jax.ShapeDtypeStruct(q.shape, BF16),
        compiler_params=_params(("parallel", "arbitrary")),
        name="swa_prompt",
    )(sinks, q, k, k, vt, vt, bias)


def _swa_sample_kernel(sink_ref, q_ref, k_ref, v_ref, bias_ref, o_ref, *, kv_heads, group, t_len):
    w = WINDOW
    for kv in range(kv_heads):
        q = jnp.concatenate(
            [q_ref[0, :, (kv * group + g) * LANES:(kv * group + g + 1) * LANES]
             for g in range(group)], axis=0).astype(BF16)
        kk = k_ref[0, :, kv * LANES:(kv + 1) * LANES]
        vv = v_ref[0, :, kv * LANES:(kv + 1) * LANES]
        s = lax.dot_general(q, kk, _NT, preferred_element_type=F32)
        s = s + bias_ref[kv * group:(kv + 1) * group].reshape(group * t_len, 2 * w)
        row = lax.broadcasted_iota(jnp.int32, s.shape, 0) & (t_len - 1)
        col = lax.broadcasted_iota(jnp.int32, s.shape, 1)
        dist = w + row - col
        s = jnp.where((dist >= 0) & (dist < w), s, NEG)
        sink_col = jnp.concatenate(
            [jnp.full((t_len, 1), sink_ref[kv * group + g], F32) for g in range(group)], axis=0)
        o = jnp.dot(_sink_softmax(s, sink_col), vv, preferred_element_type=F32)
        for g in range(group):
            hh = kv * group + g
            o_ref[0, :, hh * LANES:(hh + 1) * LANES] = o[g * t_len:(g + 1) * t_len]


def _swa_sample(q, kk, vv, sinks, bias, kv_heads, group):
    bsz, t_len, d = q.shape
    n_keys = kk.shape[1]
    return pl.pallas_call(
        functools.partial(_swa_sample_kernel, kv_heads=kv_heads, group=group, t_len=t_len),
        grid=(bsz,),
        in_specs=[pl.BlockSpec(memory_space=pltpu.SMEM),
                  pl.BlockSpec((1, t_len, d), lambda b: (b, 0, 0)),
                  pl.BlockSpec((1, n_keys, kv_heads * LANES), lambda b: (b, 0, 0)),
                  pl.BlockSpec((1, n_keys, kv_heads * LANES), lambda b: (b, 0, 0)),
                  pl.BlockSpec(bias.shape, lambda b: (0, 0, 0))],
        out_specs=pl.BlockSpec((1, t_len, d), lambda b: (b, 0, 0)),
        out_shape=jax.ShapeDtypeStruct(q.shape, F32),
        compiler_params=_params(("parallel",)),
        name="swa_sample",
    )(sinks, q, kk, vv, bias)


def _sample_bias_tiles(rel_t, n_heads, t_len):
    cols = n_heads * 2 * t_len

    def tile(per_key):
        n_keys = per_key.shape[-1]
        t = jnp.transpose(per_key, (2, 0, 1))[:, :, None, :]
        return jnp.broadcast_to(t, (n_keys, n_heads, 2, t_len)).reshape(n_keys, cols)

    last = tile(rel_t[:, :, :PAGE_SIZE])
    zero = jnp.zeros(((PAGES_PER_STEP - 1) * PAGE_SIZE, cols), F32)
    steps = jnp.stack([jnp.zeros((PAGES_PER_STEP * PAGE_SIZE, cols), F32),
                       jnp.concatenate([zero, last], axis=0)])
    new = tile(rel_t[:, :, PAGE_SIZE:PAGE_SIZE + t_len])
    new = jnp.concatenate([new, jnp.full((PAGE_SIZE - t_len, cols), NEG, F32)], axis=0)
    return steps, new


def _block_diag_queries(q, n_heads, dqk):
    bsz, t_len, _ = q.shape
    qt = jnp.transpose(q.reshape(bsz, t_len, n_heads, 2, dqk), (0, 2, 3, 4, 1))
    comp = jnp.eye(2, dtype=q.dtype)
    blk = jnp.einsum("bhcdt,ce->bhcdet", qt, comp).reshape(bsz, n_heads, 2 * dqk, 2 * t_len)
    head = jnp.eye(n_heads, dtype=q.dtype)
    full = jnp.einsum("bhdk,hg->bhdgk", blk, head)
    return full.reshape(bsz, n_heads * 2 * dqk, n_heads * 2 * t_len)


def _trunk(x, sample, caches, bias, weights, dims):
    (cache_a_k, cache_a_v, cache_b_k, cache_b_v, page_table) = caches
    (norm_ffn1, norm_attn, norm_ffn2, w_ffn_in, w_ffn_out, w_a_qkv, a_q_norm, a_k_norm,
     a_lambda, a_subln, w_a_o, norm_kv, w_b_kv, b_k_norm, w_b_q, b_q_norm, b_sinks, w_b_o) = weights
    depth, n_a, n_heads, kv_heads = dims
    bsz, t_len, d = x.shape
    m = bsz * t_len
    group = n_heads // kv_heads
    dqk = d // n_heads // 2
    hd = d // n_heads
    kv_w = kv_heads * hd
    tm = min(512, m)
    d_ff = w_ffn_out.shape[2]
    tf_cap = 1536 if m <= 256 else 512
    tf = max(t for t in range(LANES, tf_cap + 1, LANES) if d_ff % t == 0)
    x = x.reshape(m, d)
    a_ks, a_vs = [], []
    kv_k = kv_v = kb16 = vb16 = None
    for li in range(depth):
        if li == n_a:
            kv_k, kb16 = _proj(x, w_b_kv, n=kv_w, g=norm_kv, head_gain=b_k_norm, seg=hd,
                               out_dtypes=(F32, BF16), tm=tm)
            kv_v, vb16 = _proj(x, w_b_kv, n=kv_w, col_off=kv_w, g=norm_kv,
                               out_dtypes=(F32, BF16), t_rows=0 if sample else t_len, tm=tm)
        x = _ffn(x, norm_ffn1[li], w_ffn_in, w_ffn_out, li, 0, tm=tm, tf=tf)
        q_dt = F32 if sample else BF16
        if li < n_a:
            lam_init = 0.8 - 0.6 * math.exp(-0.3 * li)
            lam = _diff_lambda(a_lambda[li], lam_init)
            (q,) = _proj(x, w_a_qkv[li], n=d, g=norm_attn[li], head_gain=a_q_norm[li], seg=dqk,
                         scale=dqk ** -0.5 * LOG2E, out_dtypes=(q_dt,), tm=tm)
            k, k16 = _proj(x, w_a_qkv[li], n=d, col_off=d, g=norm_attn[li],
                           head_gain=a_k_norm[li], seg=dqk, out_dtypes=(F32, BF16), tm=tm)
            v, v16 = _proj(x, w_a_qkv[li], n=d, col_off=2 * d, g=norm_attn[li],
                           out_dtypes=(F32, BF16), t_rows=0 if sample else t_len, tm=tm)
            a_ks.append(k.reshape(bsz, t_len, n_heads, hd))
            a_vs.append(v.reshape(bsz, t_len, n_heads, hd))
            if sample:
                n_pool = cache_a_k.shape[1]
                qbd = _block_diag_queries(q.reshape(bsz, t_len, d), n_heads, dqk).astype(BF16)
                pad = ((0, 0), (0, PAGE_SIZE - t_len), (0, 0))
                o = _attn_a_sample(
                    qbd,
                    cache_a_k.reshape((-1,) + cache_a_k.shape[2:]),
                    cache_a_v.reshape((-1,) + cache_a_v.shape[2:]),
                    jnp.pad(k.reshape(bsz, t_len, d), pad), jnp.pad(v.reshape(bsz, t_len, d), pad),
                    page_table, li * n_pool, lam, bias["a_steps"], bias["a_new"],
                    a_subln[li], 1.0 - lam_init, n_heads, t_len)
            else:
                o = _attn_a_prompt(q.reshape(bsz, t_len, d), k16.reshape(bsz, t_len, d), v16, lam,
                                   bias["a_tail"], a_subln[li], 1.0 - lam_init, n_heads)
            (x,) = _proj(o.reshape(m, d), w_a_o[li], n=d, res=x, tm=tm)
        else:
            bi = li - n_a
            (q,) = _proj(x, w_b_q[bi], n=d, g=norm_attn[li], head_gain=b_q_norm[bi], seg=hd,
                         scale=hd ** -0.5, out_dtypes=(q_dt,), tm=tm)
            if sample:
                zpad = jnp.zeros((bsz, WINDOW - t_len, kv_w), BF16)
                kk = jnp.concatenate([cache_b_k.reshape(bsz, WINDOW, kv_w).astype(BF16),
                                      kb16.reshape(bsz, t_len, kv_w), zpad], axis=1)
                vv = jnp.concatenate([cache_b_v.reshape(bsz, WINDOW, kv_w).astype(BF16),
                                      vb16.reshape(bsz, t_len, kv_w), zpad], axis=1)
                o = _swa_sample(q.reshape(bsz, t_len, d), kk, vv, b_sinks[bi], bias["swa_s"],
                                kv_heads, group)
            else:
                o = _swa_prompt(q.reshape(bsz, t_len, d), kb16.reshape(bsz, t_len, kv_w), vb16,
                                b_sinks[bi], bias["swa_p"], kv_heads, group)
            (x,) = _proj(o.reshape(m, d), w_b_o[bi], n=d, res=x, tm=tm)
        x = _ffn(x, norm_ffn2[li], w_ffn_in, w_ffn_out, li, 1, tm=tm, tf=tf)
    kv_k = kv_k.reshape(bsz, t_len, kv_heads, hd)
    kv_v = kv_v.reshape(bsz, t_len, kv_heads, hd)
    if sample:
        kv_k = jnp.concatenate([cache_b_k, kv_k], axis=1)
        kv_v = jnp.concatenate([cache_b_v, kv_v], axis=1)
    keep = min(WINDOW, kv_k.shape[1])
    return (x.reshape(bsz, t_len, d), jnp.stack(a_ks), jnp.stack(a_vs),
            kv_k[:, -keep:], kv_v[:, -keep:])


def kernel(x_prompt, x_sample, cache_a_k, cache_a_v, cache_b_k, cache_b_v, page_table, rel_bias_table, norm_ffn1, norm_attn, norm_ffn2, w_ffn_in, w_ffn_out, w_a_qkv, a_q_norm, a_k_norm, a_lambda, a_subln, w_a_o, norm_kv, w_b_kv, b_k_norm, w_b_q, b_q_norm, b_sinks, w_b_o):
    depth = norm_ffn1.shape[0]
    n_a = w_a_qkv.shape[0]
    n_heads = rel_bias_table.shape[1]
    kv_heads = cache_b_k.shape[2]
    t_len = x_sample.shape[1]
    dims = (depth, n_a, n_heads, kv_heads)

    tq = TQ_A
    kc = np.arange(2 * tq)[:, None]
    qr = np.arange(tq)[None, :]
    tail0 = _bias_tiles(rel_bias_table, _bucket_np(qr - kc), rel_far=True, scale=LOG2E)
    tail1 = _bias_tiles(rel_bias_table, _bucket_np(tq + qr - kc), rel_far=True, scale=LOG2E)
    rw = np.arange(WINDOW)[:, None]
    cw = np.arange(2 * WINDOW)[None, :]
    swa_idx = _bucket_np(WINDOW + rw - cw)
    swa_p = _bias_tiles(rel_bias_table, np.maximum(swa_idx.T, 0))
    swa_s = _bias_tiles(rel_bias_table, np.maximum(swa_idx[:t_len], 0))
    rel_t = _bias_tiles(rel_bias_table, swa_idx[:t_len], rel_far=True, scale=LOG2E)
    a_steps, a_new = _sample_bias_tiles(rel_t, n_heads, t_len)
    bias = {"a_tail": jnp.stack([tail0, tail1], axis=1), "swa_p": swa_p, "swa_s": swa_s,
            "a_steps": a_steps, "a_new": a_new}

    wcast = lambda w: w.astype(BF16)
    weights = (norm_ffn1, norm_attn, norm_ffn2, wcast(w_ffn_in), wcast(w_ffn_out), wcast(w_a_qkv),
               a_q_norm, a_k_norm, a_lambda, a_subln, wcast(w_a_o), norm_kv, wcast(w_b_kv),
               b_k_norm, wcast(w_b_q), b_q_norm, b_sinks, wcast(w_b_o))
    none5 = (None,) * 5
    y_p, ak_p, av_p, bk_p, bv_p = _trunk(x_prompt, False, none5, bias, weights, dims)
    y_s, ak_s, av_s, bk_s, bv_s = _trunk(
        x_sample, True, (cache_a_k, cache_a_v, cache_b_k, cache_b_v, page_table), bias, weights, dims)
    return (y_p, y_s, ak_p, av_p, ak_s, av_s, bk_p, bv_p, bk_s, bv_s)
```
